```python
import jax, jax.numpy as jnp
from jax import lax
import numpy as np

D_MODEL = 1024
BATCH = 32
SEQ = 2048
DEPTH = 1
DEC_BATCH = 16
DEC_SEQ = 16
PAST_LEN = 2048

CHUNK = 64
HEAD_DIM = 64
ATTN_WIDTH = D_MODEL // 2
N_Q_HEADS = ATTN_WIDTH // HEAD_DIM
N_KV_HEADS = 2
GROUP = N_Q_HEADS // N_KV_HEADS
KV_WIDTH = N_KV_HEADS * HEAD_DIM
WINDOW = 128
WINDOW_CHUNKS = WINDOW // CHUNK
CONV_CH = D_MODEL - ATTN_WIDTH
CONV_KERNEL = 31
D_FF = 4 * D_MODEL
PLE_DIM = 256
EPS = 1e-6
SPLITS = [ATTN_WIDTH, ATTN_WIDTH + KV_WIDTH, ATTN_WIDTH + 2 * KV_WIDTH, ATTN_WIDTH + 2 * KV_WIDTH + CONV_CH]
IN_WIDTH = ATTN_WIDTH + 2 * KV_WIDTH + 2 * CONV_CH

kernel_name = "hymba_swa_sink_conformer_stream_step"


def rmsnorm(x, g):
    xf = x.astype(jnp.float32)
    y = xf * lax.rsqrt(jnp.mean(xf * xf, axis=-1, keepdims=True) + EPS)
    return (y * g.astype(jnp.float32)).astype(x.dtype)


def layernorm(x, g, b):
    xf = x.astype(jnp.float32)
    mu = jnp.mean(xf, axis=-1, keepdims=True)
    xc = xf - mu
    y = xc * lax.rsqrt(jnp.mean(xc * xc, axis=-1, keepdims=True) + EPS)
    return (y * g.astype(jnp.float32) + b.astype(jnp.float32)).astype(x.dtype)


def sink_attend(q, k, v, sinks, valid):
    s = jnp.einsum('...qhgd,...khd->...hgqk', q, k, preferred_element_type=jnp.float32) * (HEAD_DIM ** -0.5)
    s = jnp.where(valid, s, -jnp.inf)
    sink = sinks.astype(jnp.float32).reshape(N_KV_HEADS, GROUP, 1, 1)
    m = jnp.maximum(jnp.max(s, axis=-1, keepdims=True), sink)
    e = jnp.exp(s - m)
    p = e / (jnp.sum(e, axis=-1, keepdims=True) + jnp.exp(sink - m))
    return jnp.einsum('...hgqk,...khd->...qhgd', p.astype(v.dtype), v)


def swa_prompt(q, k, v, sinks):
    B, S = q.shape[0], q.shape[1]
    nc = S // CHUNK
    qb = q.reshape(B, nc, CHUNK, N_KV_HEADS, GROUP, HEAD_DIM)
    pad = ((0, 0), (WINDOW, 0), (0, 0), (0, 0))
    kp = jnp.pad(k, pad).reshape(B, nc + WINDOW_CHUNKS, CHUNK, N_KV_HEADS, HEAD_DIM)
    vp = jnp.pad(v, pad).reshape(B, nc + WINDOW_CHUNKS, CHUNK, N_KV_HEADS, HEAD_DIM)
    kb = jnp.concatenate([kp[:, j:j + nc] for j in range(WINDOW_CHUNKS + 1)], axis=2)
    vb = jnp.concatenate([vp[:, j:j + nc] for j in range(WINDOW_CHUNKS + 1)], axis=2)
    key_pos = jnp.arange(nc)[:, None] * CHUNK - WINDOW + jnp.arange(WINDOW + CHUNK)[None, :]
    valid = (key_pos >= 0)[:, None, None, None, :]
    o = sink_attend(qb, kb, vb, sinks, valid)
    return o.reshape(B, S, ATTN_WIDTH)


def swa_sample(q, k, v, k_cache, v_cache, sinks):
    B, T = q.shape[0], q.shape[1]
    kf = jnp.concatenate([k_cache.astype(k.dtype), k], axis=1)
    vf = jnp.concatenate([v_cache.astype(v.dtype), v], axis=1)
    o = sink_attend(q, kf, vf, sinks, True)
    return o.reshape(B, T, ATTN_WIDTH)


def depthwise_causal(u, w, b, left):
    if left is None:
        u_pad = jnp.pad(u, ((0, 0), (CONV_KERNEL - 1, 0), (0, 0)))
    else:
        u_pad = jnp.concatenate([left.astype(u.dtype), u], axis=1)
    y = lax.conv_general_dilated(u_pad, w[:, None, :].astype(u.dtype), window_strides=(1,), padding='VALID',
                                 dimension_numbers=('NWC', 'WIO', 'NWC'), feature_group_count=CONV_CH)
    return y + b, u_pad[:, -(CONV_KERNEL - 1):]


def layer(x, p_i, k_cache, v_cache, conv_cache, prm, i):
    B, T = x.shape[0], x.shape[1]
    hn = rmsnorm(x, prm['norm_mix'][i])
    proj = hn @ prm['w_in'][i]
    q, k, v, a, g = jnp.split(proj, SPLITS, axis=-1)
    q = q.reshape(B, T, N_KV_HEADS, GROUP, HEAD_DIM)
    k = k.reshape(B, T, N_KV_HEADS, HEAD_DIM)
    v = v.reshape(B, T, N_KV_HEADS, HEAD_DIM)
    sinks = prm['sinks'][i]
    if k_cache is None:
        attn = swa_prompt(q, k, v, sinks)
        new_k, new_v = k[:, -WINDOW:], v[:, -WINDOW:]
    else:
        attn = swa_sample(q, k, v, k_cache, v_cache, sinks)
        new_k, new_v = k, v
    u = a * jax.nn.sigmoid(g)
    c, new_conv = depthwise_causal(u, prm['conv_w'][i], prm['conv_b'][i], conv_cache)
    c = jax.nn.silu(layernorm(c, prm['ln_g'][i], prm['ln_b'][i]))
    merged = jnp.concatenate([rmsnorm(attn, prm['attn_out_g'][i]), rmsnorm(c, prm['conv_out_g'][i])], axis=-1)
    x = x + merged @ prm['w_out'][i]
    hf = rmsnorm(x, prm['norm_ffn'][i])
    x = x + jnp.square(jax.nn.relu(hf @ prm['w_up'][i])) @ prm['w_down'][i]
    gate = jax.nn.sigmoid(rmsnorm(x, prm['norm_ple'][i]) @ prm['w_ple_gate'][i])
    x = x + (p_i @ prm['w_ple'][i]) * gate
    return x, new_k, new_v, new_conv


def setup_inputs(seed: int = 0) -> dict:
    key = jax.random.key(seed)
    ks = jax.random.split(key, 32)
    f32 = jnp.float32

    def nrm(k, shape, scale):
        return jax.random.normal(k, shape, f32) * scale

    def gain(k, shape):
        return 1.0 + 0.05 * jax.random.normal(k, shape, f32)

    cache_rows = min(WINDOW, PAST_LEN)
    return {
        'x_prompt': nrm(ks[0], (BATCH, SEQ, D_MODEL), 1.0),
        'x_sample': nrm(ks[1], (DEC_BATCH, DEC_SEQ, D_MODEL), 1.0),
        'p_prompt': nrm(ks[2], (DEPTH, BATCH, SEQ, PLE_DIM), 1.0),
        'p_sample': nrm(ks[3], (DEPTH, DEC_BATCH, DEC_SEQ, PLE_DIM), 1.0),
        'cache_k': nrm(ks[4], (DEPTH, DEC_BATCH, cache_rows, N_KV_HEADS, HEAD_DIM), 1.0),
        'cache_v': nrm(ks[5], (DEPTH, DEC_BATCH, cache_rows, N_KV_HEADS, HEAD_DIM), 1.0),
        'state_conv': nrm(ks[6], (DEPTH, DEC_BATCH, CONV_KERNEL - 1, CONV_CH), 0.5),
        'norm_mix': gain(ks[7], (DEPTH, D_MODEL)),
        'w_in': nrm(ks[8], (DEPTH, D_MODEL, IN_WIDTH), D_MODEL ** -0.5),
        'sinks': nrm(ks[9], (DEPTH, N_Q_HEADS), 0.5),
        'conv_w': nrm(ks[10], (DEPTH, CONV_KERNEL, CONV_CH), CONV_KERNEL ** -0.5),
        'conv_b': nrm(ks[11], (DEPTH, CONV_CH), 0.02),
        'ln_g': gain(ks[12], (DEPTH, CONV_CH)),
        'ln_b': nrm(ks[13], (DEPTH, CONV_CH), 0.02),
        'attn_out_g': gain(ks[14], (DEPTH, ATTN_WIDTH)),
        'conv_out_g': gain(ks[15], (DEPTH, CONV_CH)),
        'w_out': nrm(ks[16], (DEPTH, D_MODEL, D_MODEL), D_MODEL ** -0.5),
        'norm_ffn': gain(ks[17], (DEPTH, D_MODEL)),
        'w_up': nrm(ks[18], (DEPTH, D_MODEL, D_FF), D_MODEL ** -0.5),
        'w_down': nrm(ks[19], (DEPTH, D_FF, D_MODEL), D_FF ** -0.5),
        'norm_ple': gain(ks[20], (DEPTH, D_MODEL)),
        'w_ple_gate': nrm(ks[21], (DEPTH, D_MODEL, D_MODEL), D_MODEL ** -0.5),
        'w_ple': nrm(ks[22], (DEPTH, PLE_DIM, D_MODEL), PLE_DIM ** -0.5),
        'final_norm': gain(ks[23], (D_MODEL,)),
    }


def reference(x_prompt, x_sample, p_prompt, p_sample, cache_k, cache_v, state_conv,
              norm_mix, w_in, sinks, conv_w, conv_b, ln_g, ln_b, attn_out_g, conv_out_g, w_out,
              norm_ffn, w_up, w_down, norm_ple, w_ple_gate, w_ple, final_norm):
    prm = dict(norm_mix=norm_mix, w_in=w_in, sinks=sinks, conv_w=conv_w, conv_b=conv_b, ln_g=ln_g, ln_b=ln_b,
               attn_out_g=attn_out_g, conv_out_g=conv_out_g, w_out=w_out, norm_ffn=norm_ffn, w_up=w_up,
               w_down=w_down, norm_ple=norm_ple, w_ple_gate=w_ple_gate, w_ple=w_ple)
    hp, hs = x_prompt, x_sample
    kp_l, vp_l, cp_l, ks_l, vs_l, cs_l = [], [], [], [], [], []
    for i in range(DEPTH):
        hp, kp_i, vp_i, cp_i = layer(hp, p_prompt[i], None, None, None, prm, i)
        hs, ks_i, vs_i, cs_i = layer(hs, p_sample[i], cache_k[i], cache_v[i], state_conv[i], prm, i)
        kp_l.append(kp_i); vp_l.append(vp_i); cp_l.append(cp_i)
        ks_l.append(ks_i); vs_l.append(vs_i); cs_l.append(cs_i)
    y_prompt = rmsnorm(hp, final_norm)
    y_sample = rmsnorm(hs, final_norm)
    new_k_prompt = jnp.stack(kp_l)
    new_v_prompt = jnp.stack(vp_l)
    new_conv_prompt = jnp.stack(cp_l)
    new_k_sample = jnp.stack(ks_l)
    new_v_sample = jnp.stack(vs_l)
    new_conv_sample = jnp.stack(cs_l)
    return (y_prompt, y_sample, new_k_prompt, new_v_prompt, new_conv_prompt, new_k_sample, new_v_sample, new_conv_sample)
```

```python
import functools

import jax
import jax.numpy as jnp
from jax import lax
from jax.experimental import pallas as pl
from jax.experimental.pallas import tpu as pltpu

D_MODEL = 1024
HEAD_DIM = 64
ATTN_WIDTH = 512
N_Q_HEADS = 8
KV_WIDTH = 128
WINDOW = 128
CONV_CH = 512
CONV_KERNEL = 31
CONV_HIST = 32
D_FF = 4096
PLE_DIM = 256
EPS = 1e-6
IN_WIDTH = 1792
Q_OFF, K_OFF, V_OFF, A_OFF, G_OFF = 0, 512, 640, 768, 1280
LANES = 128
FF_BLOCK = 1024
VMEM_LIMIT_BYTES = 56 * 1024 * 1024


def _rms(x, g):
    ms = jnp.mean(x * x, axis=-1, keepdims=True)
    return x * lax.rsqrt(ms + EPS) * g


def _sigmoid(x):
    return 1.0 / (1.0 + jnp.exp(-x))


def _kv_variants(k, v):
    rows = k.shape[0]
    lane = lax.broadcasted_iota(jnp.int32, (rows, LANES), 1)
    lo = lane < HEAD_DIM
    k_sw = pltpu.roll(k, HEAD_DIM, axis=1)
    v_sw = pltpu.roll(v, HEAD_DIM, axis=1)
    zero = jnp.zeros_like(k)
    one = jnp.ones_like(k)
    ones_lo = jnp.where(lo, one, zero).astype(jnp.bfloat16)
    ones_hi = jnp.where(lo, zero, one).astype(jnp.bfloat16)
    kz, vz = [], []
    for h in range(2):
        for half in range(2):
            src_k = k if h == half else k_sw
            src_v = v if h == half else v_sw
            keep = lo if half == 0 else jnp.logical_not(lo)
            kz.append(jnp.where(keep, src_k, zero).astype(jnp.bfloat16))
            vpart = jnp.where(keep, src_v, zero).astype(jnp.bfloat16)
            vz.append(jnp.concatenate([vpart, ones_lo if half == 0 else ones_hi], axis=1))
    return kz, vz


def _layer_kernel(*refs, T, CH, RB, NS, KOUT, has_cache):
    it = iter(refs)
    sinks_ref = next(it)
    x_ref = next(it)
    p_ref = next(it)
    if has_cache:
        ck_ref = next(it)
        cv_ref = next(it)
        cs_ref = next(it)
    (norm_mix_ref, w_in_ref, conv_w_ref, conv_b_ref, ln_g_ref, ln_b_ref, attn_g_ref, conv_g_ref,
     w_out_ref, norm_ffn_ref, w_up_ref, w_down_ref, norm_ple_ref, w_gate_ref, w_ple_ref,
     final_norm_ref) = [next(it) for _ in range(16)]
    y_ref, nk_ref, nv_ref, nc_ref = [next(it) for _ in range(4)]
    (hn_ref, proj_ref, kz_ref, vz_ref, attn_ref, ubuf_ref, merged_ref, x1_ref, hact_ref) = [
        next(it) for _ in range(9)]

    s_idx = pl.program_id(1)
    n_rb = T // RB
    n_ch = T // CH
    win = WINDOW + CH

    @pl.when(s_idx == 0)
    def _init_history():
        if has_cache:
            kz, vz = _kv_variants(ck_ref[0], cv_ref[0])
            for i in range(4):
                kz_ref[i, 0:WINDOW, :] = kz[i]
                vz_ref[i, 0:WINDOW, :] = vz[i]
            ubuf_ref[0:CONV_HIST, :] = cs_ref[0]
        else:
            for i in range(4):
                kz_ref[i, 0:WINDOW, :] = jnp.zeros((WINDOW, LANES), jnp.bfloat16)
                vz_ref[i, 0:WINDOW, :] = jnp.zeros((WINDOW, 2 * LANES), jnp.bfloat16)
            ubuf_ref[0:CONV_HIST, :] = jnp.zeros((CONV_HIST, CONV_CH), jnp.float32)

    def _norm_in(i, c):
        r = pl.ds(pl.multiple_of(i * RB, RB), RB)
        hn_ref[r, :] = _rms(x_ref[0, r, :], norm_mix_ref[...]).astype(jnp.bfloat16)
        return c
    lax.fori_loop(0, n_rb, _norm_in, 0)

    proj_ref[...] = jnp.dot(hn_ref[...], w_in_ref[...], preferred_element_type=jnp.float32)

    def _prep(i, c):
        r0 = pl.multiple_of(i * RB, RB)
        r = pl.ds(r0, RB)
        kz, vz = _kv_variants(proj_ref[r, K_OFF:K_OFF + KV_WIDTH], proj_ref[r, V_OFF:V_OFF + KV_WIDTH])
        rk = pl.ds(pl.multiple_of(WINDOW + r0, RB), RB)
        for j in range(4):
            kz_ref[j, rk, :] = kz[j]
            vz_ref[j, rk, :] = vz[j]
        a = proj_ref[r, A_OFF:A_OFF + CONV_CH]
        g = proj_ref[r, G_OFF:G_OFF + CONV_CH]
        ubuf_ref[pl.ds(pl.multiple_of(CONV_HIST + r0, RB), RB), :] = a * _sigmoid(g)
        return c
    lax.fori_loop(0, n_rb, _prep, 0)

    @pl.when(s_idx == NS - 1)
    def _emit_state():
        nk_ref[0] = proj_ref[T - KOUT:T, K_OFF:K_OFF + KV_WIDTH]
        nv_ref[0] = proj_ref[T - KOUT:T, V_OFF:V_OFF + KV_WIDTH]
        nc_ref[0] = ubuf_ref[T:T + CONV_HIST, :]

    lane = lax.broadcasted_iota(jnp.int32, (CH, LANES), 1)
    lo = lane < HEAD_DIM
    key_idx = lax.broadcasted_iota(jnp.int32, (CH, win), 1)

    def _attend(c, carry):
        r0 = pl.multiple_of(c * CH, CH)
        rq = pl.ds(r0, CH)
        rw = pl.ds(r0, win)
        if not has_cache:
            first_valid = WINDOW - (s_idx * T + c * CH)
            valid = key_idx >= first_valid
        for j in range(4):
            h = j // 2
            q_blk = (proj_ref[rq, Q_OFF + j * LANES:Q_OFF + (j + 1) * LANES] * (HEAD_DIM ** -0.5)
                     ).astype(jnp.bfloat16)
            acc = None
            sink_terms = []
            for half in range(2):
                kzw = kz_ref[2 * h + half, rw, :]
                s = lax.dot_general(q_blk, kzw, (((1,), (1,)), ((), ())),
                                    preferred_element_type=jnp.float32)
                if not has_cache:
                    s = jnp.where(valid, s, -jnp.inf)
                sink = sinks_ref[2 * j + half]
                m = jnp.maximum(jnp.max(s, axis=-1, keepdims=True), sink)
                e = jnp.exp(s - m).astype(jnp.bfloat16)
                part = jnp.dot(e, vz_ref[2 * h + half, rw, :], preferred_element_type=jnp.float32)
                acc = part if acc is None else acc + part
                sink_terms.append(jnp.exp(sink - m))
            den = acc[:, LANES:] + jnp.where(lo, sink_terms[0], sink_terms[1])
            attn_ref[rq, j * LANES:(j + 1) * LANES] = acc[:, :LANES] / den
        return carry
    lax.fori_loop(0, n_ch, _attend, 0)

    for i in range(n_rb):
        r0 = i * RB
        r = pl.ds(r0, RB)
        acc = jnp.zeros((RB // 8, 8, CONV_CH), jnp.float32)
        for j in range(CONV_KERNEL):
            u = ubuf_ref[pl.ds(r0 + (CONV_HIST - (CONV_KERNEL - 1)) + j, RB), :]
            acc = acc + u.reshape(RB // 8, 8, CONV_CH) * conv_w_ref[j]
        cv = acc.reshape(RB, CONV_CH) + conv_b_ref[...]
        mu = jnp.mean(cv, axis=-1, keepdims=True)
        xc = cv - mu
        y = xc * lax.rsqrt(jnp.mean(xc * xc, axis=-1, keepdims=True) + EPS) * ln_g_ref[...] + ln_b_ref[...]
        y = y * _sigmoid(y)
        merged_ref[r, ATTN_WIDTH:] = _rms(y, conv_g_ref[...]).astype(jnp.bfloat16)
        merged_ref[r, :ATTN_WIDTH] = _rms(attn_ref[r, :], attn_g_ref[...]).astype(jnp.bfloat16)

    x1_ref[...] = x_ref[0] + jnp.dot(merged_ref[...], w_out_ref[...], preferred_element_type=jnp.float32)

    def _norm_ffn(i, c):
        r = pl.ds(pl.multiple_of(i * RB, RB), RB)
        hn_ref[r, :] = _rms(x1_ref[r, :], norm_ffn_ref[...]).astype(jnp.bfloat16)
        return c
    lax.fori_loop(0, n_rb, _norm_ffn, 0)

    for f in range(D_FF // FF_BLOCK):
        cols = slice(f * FF_BLOCK, (f + 1) * FF_BLOCK)
        hid = jnp.dot(hn_ref[...], w_up_ref[:, cols], preferred_element_type=jnp.float32)
        hid = jnp.maximum(hid, 0.0)
        hact_ref[:, cols] = (hid * hid).astype(jnp.bfloat16)
    x1_ref[...] = x1_ref[...] + jnp.dot(hact_ref[...], w_down_ref[...], preferred_element_type=jnp.float32)

    def _norm_ple(i, c):
        r = pl.ds(pl.multiple_of(i * RB, RB), RB)
        hn_ref[r, :] = _rms(x1_ref[r, :], norm_ple_ref[...]).astype(jnp.bfloat16)
        return c
    lax.fori_loop(0, n_rb, _norm_ple, 0)

    gate = _sigmoid(jnp.dot(hn_ref[...], w_gate_ref[...], preferred_element_type=jnp.float32))
    ple = jnp.dot(p_ref[0].astype(jnp.bfloat16), w_ple_ref[...], preferred_element_type=jnp.float32)
    x1_ref[...] = x1_ref[...] + ple * gate

    def _norm_out(i, c):
        r = pl.ds(pl.multiple_of(i * RB, RB), RB)
        y_ref[0, r, :] = _rms(x1_ref[r, :], final_norm_ref[...])
        return c
    lax.fori_loop(0, n_rb, _norm_out, 0)

    if NS > 1:
        for i in range(4):
            kz_ref[i, 0:WINDOW, :] = kz_ref[i, T:T + WINDOW, :]
            vz_ref[i, 0:WINDOW, :] = vz_ref[i, T:T + WINDOW, :]
        ubuf_ref[0:CONV_HIST, :] = ubuf_ref[T:T + CONV_HIST, :]


def _const_spec(shape):
    nd = len(shape)
    return pl.BlockSpec(shape, lambda b, s: (0,) * nd, pipeline_mode=pl.Buffered(1))


def _layer_call(x, p, cache, weights, sinks, *, T, CH, name):
    B, S, _ = x.shape
    NS = S // T
    RB = min(64, T)
    KOUT = min(WINDOW, S)
    has_cache = cache is not None
    assert S % T == 0 and T % CH == 0 and T % RB == 0 and KOUT <= T

    seq_spec = lambda width: pl.BlockSpec((1, T, width), lambda b, s: (b, s, 0))
    per_seq = lambda rows, width: pl.BlockSpec((1, rows, width), lambda b, s: (b, 0, 0))

    in_specs = [pl.BlockSpec(memory_space=pltpu.SMEM), seq_spec(D_MODEL), seq_spec(PLE_DIM)]
    args = [sinks, x, p]
    if has_cache:
        in_specs += [per_seq(WINDOW, KV_WIDTH), per_seq(WINDOW, KV_WIDTH), per_seq(CONV_HIST, CONV_CH)]
        args += list(cache)
    in_specs += [_const_spec(w.shape) for w in weights]
    args += list(weights)

    out_shape = (
        jax.ShapeDtypeStruct((B, S, D_MODEL), jnp.float32),
        jax.ShapeDtypeStruct((B, KOUT, KV_WIDTH), jnp.float32),
        jax.ShapeDtypeStruct((B, KOUT, KV_WIDTH), jnp.float32),
        jax.ShapeDtypeStruct((B, CONV_HIST, CONV_CH), jnp.float32),
    )
    out_specs = (seq_spec(D_MODEL), per_seq(KOUT, KV_WIDTH), per_seq(KOUT, KV_WIDTH),
                 per_seq(CONV_HIST, CONV_CH))
    scratch = [
        pltpu.VMEM((T, D_MODEL), jnp.bfloat16),
        pltpu.VMEM((T, IN_WIDTH), jnp.float32),
        pltpu.VMEM((4, WINDOW + T, LANES), jnp.bfloat16),
        pltpu.VMEM((4, WINDOW + T, 2 * LANES), jnp.bfloat16),
        pltpu.VMEM((T, ATTN_WIDTH), jnp.float32),
        pltpu.VMEM((CONV_HIST + T, CONV_CH), jnp.float32),
        pltpu.VMEM((T, D_MODEL), jnp.bfloat16),
        pltpu.VMEM((T, D_MODEL), jnp.float32),
        pltpu.VMEM((T, D_FF), jnp.bfloat16),
    ]
    body = functools.partial(_layer_kernel, T=T, CH=CH, RB=RB, NS=NS, KOUT=KOUT, has_cache=has_cache)
    return pl.pallas_call(
        body,
        grid=(B, NS),
        in_specs=in_specs,
        out_specs=out_specs,
        out_shape=out_shape,
        scratch_shapes=scratch,
        compiler_params=pltpu.CompilerParams(
            dimension_semantics=("arbitrary", "arbitrary"),
            vmem_limit_bytes=VMEM_LIMIT_BYTES),
        name=name,
    )(*args)


def _pick_tile(S, target):
    t = min(S, target)
    while S % t:
        t //= 2
    return t


def kernel(x_prompt, x_sample, p_prompt, p_sample, cache_k, cache_v, state_conv, norm_mix, w_in, sinks, conv_w, conv_b, ln_g, ln_b, attn_out_g, conv_out_g, w_out, norm_ffn, w_up, w_down, norm_ple, w_ple_gate, w_ple, final_norm):
    depth = w_in.shape[0]
    assert depth == 1, "single-layer step"
    bf = jnp.bfloat16
    row = lambda v: v.reshape(1, -1)
    conv_w8 = jnp.broadcast_to(conv_w[0][:, None, :], (CONV_KERNEL, 8, CONV_CH))
    weights = (row(norm_mix[0]), w_in[0].astype(bf), conv_w8, row(conv_b[0]), row(ln_g[0]), row(ln_b[0]),
               row(attn_out_g[0]), row(conv_out_g[0]), w_out[0].astype(bf), row(norm_ffn[0]),
               w_up[0].astype(bf), w_down[0].astype(bf), row(norm_ple[0]), w_ple_gate[0].astype(bf),
               w_ple[0].astype(bf), row(final_norm))
    sk = sinks[0]

    Bp, Sp, _ = x_prompt.shape
    yp, nkp, nvp, ncp = _layer_call(x_prompt, p_prompt[0], None, weights, sk,
                                    T=_pick_tile(Sp, 256), CH=64, name="prompt_layer")

    Bs, Ss, _ = x_sample.shape
    n_cache = cache_k.shape[2]
    assert n_cache == WINDOW
    cache = (cache_k[0].reshape(Bs, WINDOW, KV_WIDTH), cache_v[0].reshape(Bs, WINDOW, KV_WIDTH),
             jnp.pad(state_conv[0], ((0, 0), (CONV_HIST - (CONV_KERNEL - 1), 0), (0, 0))))
    ys, nks, nvs, ncs = _layer_call(x_sample, p_sample[0], cache, weights, sk,
                                    T=Ss, CH=Ss, name="sample_layer")

    hist = CONV_HIST - (CONV_KERNEL - 1)
    kv_shape = lambda a: a.reshape(1, a.shape[0], a.shape[1], 2, HEAD_DIM)
    return (yp, ys, kv_shape(nkp), kv_shape(nvp), ncp[None, :, hist:, :],
            kv_shape(nks), kv_shape(nvs), ncs[None, :, hist:, :])
```

```python
import functools

import jax
import jax.numpy as jnp
from jax import lax
from jax.experimental import pallas as pl
from jax.experimental.pallas import tpu as pltpu

D_MODEL = 1024
HEAD_DIM = 64
ATTN_WIDTH = 512
KV_WIDTH = 128
WINDOW = 128
CONV_CH = 512
CONV_KERNEL = 31
CONV_HIST = 32
CONV_PAD = CONV_HIST - (CONV_KERNEL - 1)
D_FF = 4096
PLE_DIM = 256
EPS = 1e-6
Q_OFF, K_OFF, A_OFF = 0, 512, 768
LANES = 128
SUBLANES = 8
N_SLABS = CONV_CH // LANES
FF_BLOCK = 1024
FF_SPLIT = 2
VMEM_LIMIT_BYTES = 56 * 1024 * 1024


def _rms(x, g):
    ms = jnp.mean(x * x, axis=-1, keepdims=True)
    return x * lax.rsqrt(ms + EPS) * g


def _sigmoid(x):
    return 1.0 / (1.0 + jnp.exp(-x))


def _kv_variants(k, v):
    rows = k.shape[0]
    lane = lax.broadcasted_iota(jnp.int32, (rows, LANES), 1)
    lo = lane < HEAD_DIM
    k_sw = pltpu.roll(k, HEAD_DIM, axis=1)
    v_sw = pltpu.roll(v, HEAD_DIM, axis=1)
    zero = jnp.zeros_like(k)
    one = jnp.ones_like(k)
    ones_lo = jnp.where(lo, one, zero).astype(jnp.bfloat16)
    ones_hi = jnp.where(lo, zero, one).astype(jnp.bfloat16)
    kz, vz = [], []
    for h in range(2):
        for half in range(2):
            src_k = k if h == half else k_sw
            src_v = v if h == half else v_sw
            keep = lo if half == 0 else jnp.logical_not(lo)
            kz.append(jnp.where(keep, src_k, zero).astype(jnp.bfloat16))
            vpart = jnp.where(keep, src_v, zero).astype(jnp.bfloat16)
            vz.append(jnp.concatenate([vpart, ones_lo if half == 0 else ones_hi], axis=1))
    return kz, vz


def _layer_kernel(*refs, T, CH, BLK, NS, KOUT, has_cache):
    it = iter(refs)
    sinks_ref = next(it)
    x_ref = next(it)
    p_ref = next(it)
    if has_cache:
        ck_ref = next(it)
        cv_ref = next(it)
        cs_ref = next(it)
    (norm_mix_ref, w_in_ref, conv_w_ref, conv_b_ref, ln_g_ref, ln_b_ref, attn_g_ref, conv_g_ref,
     w_out_ref, norm_ffn_ref, w_up_ref, w_down_ref, norm_ple_ref, w_gate_ref, w_ple_ref,
     final_norm_ref) = [next(it) for _ in range(16)]
    y_ref, nk_ref, nv_ref, nc_ref = [next(it) for _ in range(4)]
    (hn_ref, q_ref, kv_ref, kz_ref, vz_ref, ubuf_ref, e_ref, tt_ref, cbuf_ref, merged_ref, x1_ref,
     hact_ref) = [next(it) for _ in range(12)]

    s_idx = pl.program_id(1)
    n_blk = T // BLK
    cpb = BLK // CH
    win = WINDOW + CH
    grp = min(32, BLK)
    stride = grp // SUBLANES
    unroll = n_blk <= 4

    @pl.when(s_idx == 0)
    def _init_history():
        if has_cache:
            kz, vz = _kv_variants(ck_ref[0], cv_ref[0])
            for i in range(4):
                kz_ref[i, 0:WINDOW, :] = kz[i]
                vz_ref[i, 0:WINDOW, :] = vz[i]
            for sl in range(N_SLABS):
                ubuf_ref[sl, 0:CONV_HIST, :] = cs_ref[0, :, sl * LANES:(sl + 1) * LANES]
        else:
            for i in range(4):
                kz_ref[i, 0:WINDOW, :] = jnp.zeros((WINDOW, LANES), jnp.bfloat16)
                vz_ref[i, 0:WINDOW, :] = jnp.zeros((WINDOW, 2 * LANES), jnp.bfloat16)
            for sl in range(N_SLABS):
                ubuf_ref[sl, 0:CONV_HIST, :] = jnp.zeros((CONV_HIST, LANES), jnp.float32)

    def _rows(i):
        return pl.ds(pl.multiple_of(i * BLK, BLK), BLK)

    def _norm_in(i, c):
        r = _rows(i)
        hn_ref[r, :] = _rms(x_ref[0, r, :], norm_mix_ref[...]).astype(jnp.bfloat16)
        return c
    lax.fori_loop(0, n_blk, _norm_in, 0, unroll=unroll)

    q_ref[...] = (jnp.dot(hn_ref[...], w_in_ref[:, Q_OFF:Q_OFF + ATTN_WIDTH],
                          preferred_element_type=jnp.float32) * (HEAD_DIM ** -0.5)).astype(jnp.bfloat16)
    kv_ref[...] = jnp.dot(hn_ref[...], w_in_ref[:, K_OFF:K_OFF + 2 * KV_WIDTH],
                          preferred_element_type=jnp.float32)
    ag = jnp.dot(hn_ref[...], w_in_ref[:, A_OFF:], preferred_element_type=jnp.float32)
    for sl in range(N_SLABS):
        a = ag[:, sl * LANES:(sl + 1) * LANES]
        g = ag[:, CONV_CH + sl * LANES:CONV_CH + (sl + 1) * LANES]
        ubuf_ref[sl, CONV_HIST:CONV_HIST + T, :] = a * _sigmoid(g)

    def _prep(i, c):
        r0 = pl.multiple_of(i * BLK, BLK)
        kz, vz = _kv_variants(kv_ref[pl.ds(r0, BLK), 0:KV_WIDTH], kv_ref[pl.ds(r0, BLK), KV_WIDTH:])
        rk = pl.ds(pl.multiple_of(WINDOW + r0, BLK), BLK)
        for j in range(4):
            kz_ref[j, rk, :] = kz[j]
            vz_ref[j, rk, :] = vz[j]
        return c
    lax.fori_loop(0, n_blk, _prep, 0, unroll=unroll)

    @pl.when(s_idx == NS - 1)
    def _emit_state():
        nk_ref[0] = kv_ref[T - KOUT:T, 0:KV_WIDTH]
        nv_ref[0] = kv_ref[T - KOUT:T, KV_WIDTH:]
        nc_ref[0] = jnp.concatenate([ubuf_ref[sl, T:T + CONV_HIST, :] for sl in range(N_SLABS)], axis=1)

    lane = lax.broadcasted_iota(jnp.int32, (2 * CH, LANES), 1)
    lo = lane < HEAD_DIM
    key_idx = lax.broadcasted_iota(jnp.int32, (2 * CH, win), 1)
    top_rows = lax.broadcasted_iota(jnp.int32, (2 * CH, 1), 0) < CH

    def _scores_conv(i, carry):
        for ci in range(cpb):
            c = i * cpb + ci
            r0 = pl.multiple_of(c * CH, CH)
            q_c = q_ref[pl.ds(r0, CH), :]
            if not has_cache:
                valid = key_idx >= WINDOW - (s_idx * T + c * CH)
            for h in range(2):
                lhs = jnp.concatenate([q_c[:, (2 * h) * LANES:(2 * h + 1) * LANES],
                                       q_c[:, (2 * h + 1) * LANES:(2 * h + 2) * LANES]], axis=0)
                sink_terms = []
                for half in range(2):
                    s = lax.dot_general(lhs, kz_ref[2 * h + half, pl.ds(r0, win), :],
                                        (((1,), (1,)), ((), ())), preferred_element_type=jnp.float32)
                    if not has_cache:
                        s = jnp.where(valid, s, -jnp.inf)
                    sink = jnp.where(top_rows, sinks_ref[4 * h + half], sinks_ref[4 * h + 2 + half])
                    m = jnp.maximum(jnp.max(s, axis=-1, keepdims=True), sink)
                    e_ref[(c * 2 + h) * 2 + half] = jnp.exp(s - m).astype(jnp.bfloat16)
                    sink_terms.append(jnp.exp(sink - m))
                tt_ref[c * 2 + h] = jnp.where(lo, sink_terms[0], sink_terms[1])

        b0 = i * BLK
        for sl in range(N_SLABS):
            cols = slice(sl * LANES, (sl + 1) * LANES)
            starts = [g0 + v for g0 in range(0, BLK, grp) for v in range(stride)]
            acc = [jnp.zeros((SUBLANES, LANES), jnp.float32) for _ in starts]
            for j in range(CONV_KERNEL):
                w = conv_w_ref[j, :, cols]
                for n, st in enumerate(starts):
                    acc[n] = acc[n] + w * ubuf_ref[sl, pl.ds(b0 + st + CONV_PAD + j, SUBLANES, stride=stride), :]
            bias = conv_b_ref[:, cols]
            for n, st in enumerate(starts):
                cbuf_ref[sl, pl.ds(b0 + st, SUBLANES, stride=stride), :] = acc[n] + bias
        return carry
    lax.fori_loop(0, n_blk, _scores_conv, 0)

    def _mix(i, carry):
        for ci in range(cpb):
            c = i * cpb + ci
            r0 = pl.multiple_of(c * CH, CH)
            blocks = [None] * 4
            for h in range(2):
                acc = None
                for half in range(2):
                    part = jnp.dot(e_ref[(c * 2 + h) * 2 + half], vz_ref[2 * h + half, pl.ds(r0, win), :],
                                   preferred_element_type=jnp.float32)
                    acc = part if acc is None else acc + part
                o = acc[:, :LANES] / (acc[:, LANES:] + tt_ref[c * 2 + h])
                blocks[2 * h] = o[:CH]
                blocks[2 * h + 1] = o[CH:]
            attn = jnp.concatenate(blocks, axis=1)
            merged_ref[pl.ds(r0, CH), :ATTN_WIDTH] = _rms(attn, attn_g_ref[...]).astype(jnp.bfloat16)
        r = _rows(i)
        cv = jnp.concatenate([cbuf_ref[sl, r, :] for sl in range(N_SLABS)], axis=1)
        mu = jnp.mean(cv, axis=-1, keepdims=True)
        xc = cv - mu
        y = xc * lax.rsqrt(jnp.mean(xc * xc, axis=-1, keepdims=True) + EPS) * ln_g_ref[...] + ln_b_ref[...]
        y = y * _sigmoid(y)
        merged_ref[r, ATTN_WIDTH:] = _rms(y, conv_g_ref[...]).astype(jnp.bfloat16)
        return carry
    lax.fori_loop(0, n_blk, _mix, 0)

    x1_ref[...] = x_ref[0] + jnp.dot(merged_ref[...], w_out_ref[...], preferred_element_type=jnp.float32)

    def _norm_ffn(i, c):
        r = _rows(i)
        hn_ref[r, :] = _rms(x1_ref[r, :], norm_ffn_ref[...]).astype(jnp.bfloat16)
        return c
    lax.fori_loop(0, n_blk, _norm_ffn, 0, unroll=unroll)

    ff_group = D_FF // FF_SPLIT
    for fs in range(FF_SPLIT):
        for f in range(ff_group // FF_BLOCK):
            c0 = fs * ff_group + f * FF_BLOCK
            hid = jnp.dot(hn_ref[...], w_up_ref[:, c0:c0 + FF_BLOCK], preferred_element_type=jnp.float32)
            hid = jnp.maximum(hid, 0.0)
            hact_ref[:, f * FF_BLOCK:(f + 1) * FF_BLOCK] = (hid * hid).astype(jnp.bfloat16)
        x1_ref[...] = x1_ref[...] + jnp.dot(hact_ref[...], w_down_ref[fs * ff_group:(fs + 1) * ff_group, :],
                                            preferred_element_type=jnp.float32)

    def _norm_ple(i, c):
        r = _rows(i)
        hn_ref[r, :] = _rms(x1_ref[r, :], norm_ple_ref[...]).astype(jnp.bfloat16)
        return c
    lax.fori_loop(0, n_blk, _norm_ple, 0, unroll=unroll)

    gate = _sigmoid(jnp.dot(hn_ref[...], w_gate_ref[...], preferred_element_type=jnp.float32))
    ple = jnp.dot(p_ref[0].astype(jnp.bfloat16), w_ple_ref[...], preferred_element_type=jnp.float32)
    x1_ref[...] = x1_ref[...] + ple * gate

    def _norm_out(i, c):
        r = _rows(i)
        y_ref[0, r, :] = _rms(x1_ref[r, :], final_norm_ref[...])
        return c
    lax.fori_loop(0, n_blk, _norm_out, 0, unroll=unroll)

    if NS > 1:
        for i in range(4):
            kz_ref[i, 0:WINDOW, :] = kz_ref[i, T:T + WINDOW, :]
            vz_ref[i, 0:WINDOW, :] = vz_ref[i, T:T + WINDOW, :]
        for sl in range(N_SLABS):
            ubuf_ref[sl, 0:CONV_HIST, :] = ubuf_ref[sl, T:T + CONV_HIST, :]


def _const_spec(shape):
    nd = len(shape)
    return pl.BlockSpec(shape, lambda b, s: (0,) * nd, pipeline_mode=pl.Buffered(1))


def _layer_call(x, p, cache, weights, sinks, *, T, CH, name):
    B, S, _ = x.shape
    NS = S // T
    BLK = min(128, T)
    KOUT = min(WINDOW, S)
    has_cache = cache is not None
    assert S % T == 0 and T % BLK == 0 and BLK % CH == 0 and KOUT <= T and T >= CONV_HIST // 2
    n_ch = T // CH
    win = WINDOW + CH

    seq_spec = lambda width: pl.BlockSpec((1, T, width), lambda b, s: (b, s, 0))
    per_seq = lambda rows, width: pl.BlockSpec((1, rows, width), lambda b, s: (b, 0, 0))

    in_specs = [pl.BlockSpec(memory_space=pltpu.SMEM), seq_spec(D_MODEL), seq_spec(PLE_DIM)]
    args = [sinks, x, p]
    if has_cache:
        in_specs += [per_seq(WINDOW, KV_WIDTH), per_seq(WINDOW, KV_WIDTH), per_seq(CONV_HIST, CONV_CH)]
        args += list(cache)
    in_specs += [_const_spec(w.shape) for w in weights]
    args += list(weights)

    out_shape = (
        jax.ShapeDtypeStruct((B, S, D_MODEL), jnp.float32),
        jax.ShapeDtypeStruct((B, KOUT, KV_WIDTH), jnp.float32),
        jax.ShapeDtypeStruct((B, KOUT, KV_WIDTH), jnp.float32),
        jax.ShapeDtypeStruct((B, CONV_HIST, CONV_CH), jnp.float32),
    )
    out_specs = (seq_spec(D_MODEL), per_seq(KOUT, KV_WIDTH), per_seq(KOUT, KV_WIDTH),
                 per_seq(CONV_HIST, CONV_CH))
    scratch = [
        pltpu.VMEM((T, D_MODEL), jnp.bfloat16),
        pltpu.VMEM((T, ATTN_WIDTH), jnp.bfloat16),
        pltpu.VMEM((T, 2 * KV_WIDTH), jnp.float32),
        pltpu.VMEM((4, WINDOW + T, LANES), jnp.bfloat16),
        pltpu.VMEM((4, WINDOW + T, 2 * LANES), jnp.bfloat16),
        pltpu.VMEM((N_SLABS, CONV_HIST + T, LANES), jnp.float32),
        pltpu.VMEM((4 * n_ch, 2 * CH, win), jnp.bfloat16),
        pltpu.VMEM((2 * n_ch, 2 * CH, LANES), jnp.float32),
        pltpu.VMEM((N_SLABS, T, LANES), jnp.float32),
        pltpu.VMEM((T, D_MODEL), jnp.bfloat16),
        pltpu.VMEM((T, D_MODEL), jnp.float32),
        pltpu.VMEM((T, D_FF // FF_SPLIT), jnp.bfloat16),
    ]
    body = functools.partial(_layer_kernel, T=T, CH=CH, BLK=BLK, NS=NS, KOUT=KOUT, has_cache=has_cache)
    return pl.pallas_call(
        body,
        grid=(B, NS),
        in_specs=in_specs,
        out_specs=out_specs,
        out_shape=out_shape,
        scratch_shapes=scratch,
        compiler_params=pltpu.CompilerParams(
            dimension_semantics=("arbitrary", "arbitrary"),
            vmem_limit_bytes=VMEM_LIMIT_BYTES),
        name=name,
    )(*args)


def _pick_tile(S, target):
    t = min(S, target)
    while S % t:
        t //= 2
    return t


def kernel(x_prompt, x_sample, p_prompt, p_sample, cache_k, cache_v, state_conv, norm_mix, w_in, sinks, conv_w, conv_b, ln_g, ln_b, attn_out_g, conv_out_g, w_out, norm_ffn, w_up, w_down, norm_ple, w_ple_gate, w_ple, final_norm):
    depth = w_in.shape[0]
    assert depth == 1, "single-layer step"
    bf = jnp.bfloat16
    row = lambda v: v.reshape(1, -1)
    conv_w8 = jnp.broadcast_to(conv_w[0][:, None, :], (CONV_KERNEL, SUBLANES, CONV_CH))
    weights = (row(norm_mix[0]), w_in[0].astype(bf), conv_w8, row(conv_b[0]), row(ln_g[0]), row(ln_b[0]),
               row(attn_out_g[0]), row(conv_out_g[0]), w_out[0].astype(bf), row(norm_ffn[0]),
               w_up[0].astype(bf), w_down[0].astype(bf), row(norm_ple[0]), w_ple_gate[0].astype(bf),
               w_ple[0].astype(bf), row(final_norm))
    sk = sinks[0]

    Bp, Sp, _ = x_prompt.shape
    yp, nkp, nvp, ncp = _layer_call(x_prompt, p_prompt[0], None, weights, sk,
                                    T=_pick_tile(Sp, 512), CH=64, name="prompt_layer")

    Bs, Ss, _ = x_sample.shape
    assert cache_k.shape[2] == WINDOW
    cache = (cache_k[0].reshape(Bs, WINDOW, KV_WIDTH), cache_v[0].reshape(Bs, WINDOW, KV_WIDTH),
             jnp.pad(state_conv[0], ((0, 0), (CONV_PAD, 0), (0, 0))))
    ys, nks, nvs, ncs = _layer_call(x_sample, p_sample[0], cache, weights, sk,
                                    T=Ss, CH=Ss, name="sample_layer")

    kv_shape = lambda a: a.reshape(1, a.shape[0], a.shape[1], 2, HEAD_DIM)
    return (yp, ys, kv_shape(nkp), kv_shape(nvp), ncp[None, :, CONV_PAD:, :],
            kv_shape(nks), kv_shape(nvs), ncs[None, :, CONV_PAD:, :])
```

```python
import functools

import jax
import jax.numpy as jnp
from jax import lax
from jax.experimental import pallas as pl
from jax.experimental.pallas import tpu as pltpu

D_MODEL = 1024
HEAD_DIM = 64
ATTN_WIDTH = 512
KV_WIDTH = 128
WINDOW = 128
CONV_CH = 512
CONV_KERNEL = 31
CONV_HIST = 32
CONV_PAD = CONV_HIST - (CONV_KERNEL - 1)
D_FF = 4096
PLE_DIM = 256
EPS = 1e-6
Q_OFF, K_OFF, A_OFF = 0, 512, 768
LANES = 128
SUBLANES = 8
N_SLABS = CONV_CH // LANES
FF_BLOCK = 1024
FF_SPLIT = 2
VMEM_LIMIT_BYTES = 56 * 1024 * 1024


def _rms(x, g):
    ms = jnp.mean(x * x, axis=-1, keepdims=True)
    return x * lax.rsqrt(ms + EPS) * g


def _sigmoid(x):
    return 1.0 / (1.0 + jnp.exp(-x))


def _kv_variants(k, v):
    rows = k.shape[0]
    lane = lax.broadcasted_iota(jnp.int32, (rows, LANES), 1)
    lo = lane < HEAD_DIM
    k_sw = pltpu.roll(k, HEAD_DIM, axis=1)
    v_sw = pltpu.roll(v, HEAD_DIM, axis=1)
    zero = jnp.zeros_like(k)
    one = jnp.ones_like(k)
    ones_lo = jnp.where(lo, one, zero).astype(jnp.bfloat16)
    ones_hi = jnp.where(lo, zero, one).astype(jnp.bfloat16)
    kz, vz = [], []
    for h in range(2):
        for half in range(2):
            src_k = k if h == half else k_sw
            src_v = v if h == half else v_sw
            keep = lo if half == 0 else jnp.logical_not(lo)
            kz.append(jnp.where(keep, src_k, zero).astype(jnp.bfloat16))
            vpart = jnp.where(keep, src_v, zero).astype(jnp.bfloat16)
            vz.append(jnp.concatenate([vpart, ones_lo if half == 0 else ones_hi], axis=1))
    return kz, vz


def _layer_kernel(*refs, T, CH, BLK, HALF, NS, KOUT, has_cache):
    it = iter(refs)
    sinks_ref = next(it)
    x_ref = next(it)
    p_ref = next(it)
    if has_cache:
        ck_ref = next(it)
        cv_ref = next(it)
        cs_ref = next(it)
    (norm_mix_ref, w_in_ref, conv_w_ref, conv_b_ref, ln_g_ref, ln_b_ref, attn_g_ref, conv_g_ref,
     w_out_ref, norm_ffn_ref, w_up_ref, w_down_ref, norm_ple_ref, w_gate_ref, w_ple_ref,
     final_norm_ref) = [next(it) for _ in range(16)]
    y_ref, nk_ref, nv_ref, nc_ref = [next(it) for _ in range(4)]
    (hn_ref, q_ref, kv_ref, kz_ref, vz_ref, ubuf_ref, e_ref, tt_ref, cbuf_ref, merged_ref, x1_ref,
     hact_ref) = [next(it) for _ in range(12)]

    s_idx = pl.program_id(1)
    cpb = BLK // CH
    win = WINDOW + CH
    grp = min(32, BLK)
    stride = grp // SUBLANES

    @pl.when(s_idx == 0)
    def _init_history():
        if has_cache:
            kz, vz = _kv_variants(ck_ref[0], cv_ref[0])
            for i in range(4):
                kz_ref[i, 0:WINDOW, :] = kz[i]
                vz_ref[i, 0:WINDOW, :] = vz[i]
            for sl in range(N_SLABS):
                ubuf_ref[sl, 0:CONV_HIST, :] = cs_ref[0, :, sl * LANES:(sl + 1) * LANES]
        else:
            for i in range(4):
                kz_ref[i, 0:WINDOW, :] = jnp.zeros((WINDOW, LANES), jnp.bfloat16)
                vz_ref[i, 0:WINDOW, :] = jnp.zeros((WINDOW, 2 * LANES), jnp.bfloat16)
            for sl in range(N_SLABS):
                ubuf_ref[sl, 0:CONV_HIST, :] = jnp.zeros((CONV_HIST, LANES), jnp.float32)

    lane = lax.broadcasted_iota(jnp.int32, (2 * CH, LANES), 1)
    lo = lane < HEAD_DIM
    key_idx = lax.broadcasted_iota(jnp.int32, (2 * CH, win), 1)
    top_rows = lax.broadcasted_iota(jnp.int32, (2 * CH, 1), 0) < CH

    def _norm_to_hn(src_ref_rows, g_ref, r0, n):
        for b0 in range(r0, r0 + n, BLK):
            hn_ref[b0:b0 + BLK, :] = _rms(src_ref_rows(b0), g_ref[...]).astype(jnp.bfloat16)

    def _project(r0, n):
        r = slice(r0, r0 + n)
        q_ref[r, :] = (jnp.dot(hn_ref[r, :], w_in_ref[:, Q_OFF:Q_OFF + ATTN_WIDTH],
                               preferred_element_type=jnp.float32) * (HEAD_DIM ** -0.5)).astype(jnp.bfloat16)
        kv_ref[r, :] = jnp.dot(hn_ref[r, :], w_in_ref[:, K_OFF:K_OFF + 2 * KV_WIDTH],
                               preferred_element_type=jnp.float32)
        ag = jnp.dot(hn_ref[r, :], w_in_ref[:, A_OFF:], preferred_element_type=jnp.float32)
        for sl in range(N_SLABS):
            a = ag[:, sl * LANES:(sl + 1) * LANES]
            g = ag[:, CONV_CH + sl * LANES:CONV_CH + (sl + 1) * LANES]
            ubuf_ref[sl, CONV_HIST + r0:CONV_HIST + r0 + n, :] = a * _sigmoid(g)
        for b0 in range(r0, r0 + n, BLK):
            kz, vz = _kv_variants(kv_ref[b0:b0 + BLK, 0:KV_WIDTH], kv_ref[b0:b0 + BLK, KV_WIDTH:])
            for j in range(4):
                kz_ref[j, WINDOW + b0:WINDOW + b0 + BLK, :] = kz[j]
                vz_ref[j, WINDOW + b0:WINDOW + b0 + BLK, :] = vz[j]

    def _scores_conv(b0):
        for ci in range(cpb):
            c = b0 // CH + ci
            r0 = c * CH
            q_c = q_ref[r0:r0 + CH, :]
            if not has_cache:
                valid = key_idx >= WINDOW - (s_idx * T + r0)
            for h in range(2):
                lhs = jnp.concatenate([q_c[:, (2 * h) * LANES:(2 * h + 1) * LANES],
                                       q_c[:, (2 * h + 1) * LANES:(2 * h + 2) * LANES]], axis=0)
                sink_terms = []
                for half in range(2):
                    s = lax.dot_general(lhs, kz_ref[2 * h + half, r0:r0 + win, :],
                                        (((1,), (1,)), ((), ())), preferred_element_type=jnp.float32)
                    if not has_cache:
                        s = jnp.where(valid, s, -jnp.inf)
                    sink = jnp.where(top_rows, sinks_ref[4 * h + half], sinks_ref[4 * h + 2 + half])
                    m = jnp.maximum(jnp.max(s, axis=-1, keepdims=True), sink)
                    e_ref[(c * 2 + h) * 2 + half] = jnp.exp(s - m).astype(jnp.bfloat16)
                    sink_terms.append(jnp.exp(sink - m))
                tt_ref[c * 2 + h] = jnp.where(lo, sink_terms[0], sink_terms[1])

        for sl in range(N_SLABS):
            cols = slice(sl * LANES, (sl + 1) * LANES)
            starts = [b0 + g0 + v for g0 in range(0, BLK, grp) for v in range(stride)]
            acc = [jnp.zeros((SUBLANES, LANES), jnp.float32) for _ in starts]
            for j in range(CONV_KERNEL):
                w = conv_w_ref[j, :, cols]
                for n, st in enumerate(starts):
                    acc[n] = acc[n] + w * ubuf_ref[sl, pl.ds(st + CONV_PAD + j, SUBLANES, stride=stride), :]
            bias = conv_b_ref[:, cols]
            for n, st in enumerate(starts):
                cbuf_ref[sl, pl.ds(st, SUBLANES, stride=stride), :] = acc[n] + bias

    def _mix(b0):
        for ci in range(cpb):
            c = b0 // CH + ci
            r0 = c * CH
            blocks = [None] * 4
            for h in range(2):
                acc = None
                for half in range(2):
                    part = jnp.dot(e_ref[(c * 2 + h) * 2 + half], vz_ref[2 * h + half, r0:r0 + win, :],
                                   preferred_element_type=jnp.float32)
                    acc = part if acc is None else acc + part
                o = acc[:, :LANES] / (acc[:, LANES:] + tt_ref[c * 2 + h])
                blocks[2 * h] = o[:CH]
                blocks[2 * h + 1] = o[CH:]
            attn = jnp.concatenate(blocks, axis=1)
            merged_ref[r0:r0 + CH, :ATTN_WIDTH] = _rms(attn, attn_g_ref[...]).astype(jnp.bfloat16)
        r = slice(b0, b0 + BLK)
        cv = jnp.concatenate([cbuf_ref[sl, r, :] for sl in range(N_SLABS)], axis=1)
        mu = jnp.mean(cv, axis=-1, keepdims=True)
        xc = cv - mu
        y = xc * lax.rsqrt(jnp.mean(xc * xc, axis=-1, keepdims=True) + EPS) * ln_g_ref[...] + ln_b_ref[...]
        y = y * _sigmoid(y)
        merged_ref[r, ATTN_WIDTH:] = _rms(y, conv_g_ref[...]).astype(jnp.bfloat16)

    def _tail(r0, n):
        r = slice(r0, r0 + n)
        x1_ref[r, :] = x_ref[0, r, :] + jnp.dot(merged_ref[r, :], w_out_ref[...],
                                                preferred_element_type=jnp.float32)
        _norm_to_hn(lambda b0: x1_ref[b0:b0 + BLK, :], norm_ffn_ref, r0, n)
        ff_group = D_FF // FF_SPLIT
        for fs in range(FF_SPLIT):
            for f in range(ff_group // FF_BLOCK):
                c0 = fs * ff_group + f * FF_BLOCK
                hid = jnp.dot(hn_ref[r, :], w_up_ref[:, c0:c0 + FF_BLOCK], preferred_element_type=jnp.float32)
                hid = jnp.maximum(hid, 0.0)
                hact_ref[r, f * FF_BLOCK:(f + 1) * FF_BLOCK] = (hid * hid).astype(jnp.bfloat16)
            x1_ref[r, :] = x1_ref[r, :] + jnp.dot(hact_ref[r, :], w_down_ref[fs * ff_group:(fs + 1) * ff_group, :],
                                                  preferred_element_type=jnp.float32)
        _norm_to_hn(lambda b0: x1_ref[b0:b0 + BLK, :], norm_ple_ref, r0, n)
        gate = _sigmoid(jnp.dot(hn_ref[r, :], w_gate_ref[...], preferred_element_type=jnp.float32))
        ple = jnp.dot(p_ref[0, r, :].astype(jnp.bfloat16), w_ple_ref[...], preferred_element_type=jnp.float32)
        x1_ref[r, :] = x1_ref[r, :] + ple * gate
        for b0 in range(r0, r0 + n, BLK):
            y_ref[0, b0:b0 + BLK, :] = _rms(x1_ref[b0:b0 + BLK, :], final_norm_ref[...])

    halves = range(0, T, HALF)
    for r0 in halves:
        _norm_to_hn(lambda b0: x_ref[0, b0:b0 + BLK, :], norm_mix_ref, r0, HALF)
        _project(r0, HALF)
    for r0 in halves:
        for b0 in range(r0, r0 + HALF, BLK):
            _scores_conv(b0)
        for b0 in range(r0, r0 + HALF, BLK):
            _mix(b0)
        _tail(r0, HALF)

    @pl.when(s_idx == NS - 1)
    def _emit_state():
        nk_ref[0] = kv_ref[T - KOUT:T, 0:KV_WIDTH]
        nv_ref[0] = kv_ref[T - KOUT:T, KV_WIDTH:]
        nc_ref[0] = jnp.concatenate([ubuf_ref[sl, T:T + CONV_HIST, :] for sl in range(N_SLABS)], axis=1)

    if NS > 1:
        for i in range(4):
            kz_ref[i, 0:WINDOW, :] = kz_ref[i, T:T + WINDOW, :]
            vz_ref[i, 0:WINDOW, :] = vz_ref[i, T:T + WINDOW, :]
        for sl in range(N_SLABS):
            ubuf_ref[sl, 0:CONV_HIST, :] = ubuf_ref[sl, T:T + CONV_HIST, :]


def _const_spec(shape):
    nd = len(shape)
    return pl.BlockSpec(shape, lambda b, s: (0,) * nd, pipeline_mode=pl.Buffered(1))


def _layer_call(x, p, cache, weights, sinks, *, T, CH, name):
    B, S, _ = x.shape
    NS = S // T
    BLK = min(128, T)
    HALF = min(256, T)
    KOUT = min(WINDOW, S)
    has_cache = cache is not None
    assert S % T == 0 and T % HALF == 0 and HALF % BLK == 0 and BLK % CH == 0 and KOUT <= T
    n_ch = T // CH
    win = WINDOW + CH

    seq_spec = lambda width: pl.BlockSpec((1, T, width), lambda b, s: (b, s, 0))
    per_seq = lambda rows, width: pl.BlockSpec((1, rows, width), lambda b, s: (b, 0, 0))

    in_specs = [pl.BlockSpec(memory_space=pltpu.SMEM), seq_spec(D_MODEL), seq_spec(PLE_DIM)]
    args = [sinks, x, p]
    if has_cache:
        in_specs += [per_seq(WINDOW, KV_WIDTH), per_seq(WINDOW, KV_WIDTH), per_seq(CONV_HIST, CONV_CH)]
        args += list(cache)
    in_specs += [_const_spec(w.shape) for w in weights]
    args += list(weights)

    out_shape = (
        jax.ShapeDtypeStruct((B, S, D_MODEL), jnp.float32),
        jax.ShapeDtypeStruct((B, KOUT, KV_WIDTH), jnp.float32),
        jax.ShapeDtypeStruct((B, KOUT, KV_WIDTH), jnp.float32),
        jax.ShapeDtypeStruct((B, CONV_HIST, CONV_CH), jnp.float32),
    )
    out_specs = (seq_spec(D_MODEL), per_seq(KOUT, KV_WIDTH), per_seq(KOUT, KV_WIDTH),
                 per_seq(CONV_HIST, CONV_CH))
    scratch = [
        pltpu.VMEM((T, D_MODEL), jnp.bfloat16),
        pltpu.VMEM((T, ATTN_WIDTH), jnp.bfloat16),
        pltpu.VMEM((T, 2 * KV_WIDTH), jnp.float32),
        pltpu.VMEM((4, WINDOW + T, LANES), jnp.bfloat16),
        pltpu.VMEM((4, WINDOW + T, 2 * LANES), jnp.bfloat16),
        pltpu.VMEM((N_SLABS, CONV_HIST + T, LANES), jnp.float32),
        pltpu.VMEM((4 * n_ch, 2 * CH, win), jnp.bfloat16),
        pltpu.VMEM((2 * n_ch, 2 * CH, LANES), jnp.float32),
        pltpu.VMEM((N_SLABS, T, LANES), jnp.float32),
        pltpu.VMEM((T, D_MODEL), jnp.bfloat16),
        pltpu.VMEM((T, D_MODEL), jnp.float32),
        pltpu.VMEM((T, D_FF // FF_SPLIT), jnp.bfloat16),
    ]
    body = functools.partial(_layer_kernel, T=T, CH=CH, BLK=BLK, HALF=HALF, NS=NS, KOUT=KOUT,
                             has_cache=has_cache)
    return pl.pallas_call(
        body,
        grid=(B, NS),
        in_specs=in_specs,
        out_specs=out_specs,
        out_shape=out_shape,
        scratch_shapes=scratch,
        compiler_params=pltpu.CompilerParams(
            dimension_semantics=("arbitrary", "arbitrary"),
            vmem_limit_bytes=VMEM_LIMIT_BYTES),
        name=name,
    )(*args)


def _pick_tile(S, target):
    t = min(S, target)
    while S % t:
        t //= 2
    return t


def kernel(x_prompt, x_sample, p_prompt, p_sample, cache_k, cache_v, state_conv, norm_mix, w_in, sinks, conv_w, conv_b, ln_g, ln_b, attn_out_g, conv_out_g, w_out, norm_ffn, w_up, w_down, norm_ple, w_ple_gate, w_ple, final_norm):
    depth = w_in.shape[0]
    assert depth == 1, "single-layer step"
    bf = jnp.bfloat16
    row = lambda v: v.reshape(1, -1)
    conv_w8 = jnp.broadcast_to(conv_w[0][:, None, :], (CONV_KERNEL, SUBLANES, CONV_CH))
    weights = (row(norm_mix[0]), w_in[0].astype(bf), conv_w8, row(conv_b[0]), row(ln_g[0]), row(ln_b[0]),
               row(attn_out_g[0]), row(conv_out_g[0]), w_out[0].astype(bf), row(norm_ffn[0]),
               w_up[0].astype(bf), w_down[0].astype(bf), row(norm_ple[0]), w_ple_gate[0].astype(bf),
               w_ple[0].astype(bf), row(final_norm))
    sk = sinks[0]

    Bp, Sp, _ = x_prompt.shape
    yp, nkp, nvp, ncp = _layer_call(x_prompt, p_prompt[0], None, weights, sk,
                                    T=_pick_tile(Sp, 512), CH=64, name="prompt_layer")

    Bs, Ss, _ = x_sample.shape
    assert cache_k.shape[2] == WINDOW
    cache = (cache_k[0].reshape(Bs, WINDOW, KV_WIDTH), cache_v[0].reshape(Bs, WINDOW, KV_WIDTH),
             jnp.pad(state_conv[0], ((0, 0), (CONV_PAD, 0), (0, 0))))
    ys, nks, nvs, ncs = _layer_call(x_sample, p_sample[0], cache, weights, sk,
                                    T=Ss, CH=Ss, name="sample_layer")

    kv_shape = lambda a: a.reshape(1, a.shape[0], a.shape[1], 2, HEAD_DIM)
    return (yp, ys, kv_shape(nkp), kv_shape(nvp), ncp[None, :, CONV_PAD:, :],
            kv_shape(nks), kv_shape(nvs), ncs[None, :, CONV_PAD:, :])
```

```python
import functools

import jax
import jax.numpy as jnp
from jax import lax
from jax.experimental import pallas as pl
from jax.experimental.pallas import tpu as pltpu

D_MODEL = 1024
HEAD_DIM = 64
ATTN_WIDTH = 512
KV_WIDTH = 128
WINDOW = 128
CONV_CH = 512
CONV_KERNEL = 31
CONV_HIST = 32
CONV_PAD = CONV_HIST - (CONV_KERNEL - 1)
D_FF = 4096
PLE_DIM = 256
EPS = 1e-6
Q_OFF, K_OFF, A_OFF = 0, 512, 768
LANES = 128
SUBLANES = 8
N_SLABS = CONV_CH // LANES
FF_BLOCK = 1024
FF_SPLIT = 2
VMEM_LIMIT_BYTES = 56 * 1024 * 1024


def _rms(x, g):
    ms = jnp.mean(x * x, axis=-1, keepdims=True)
    return x * lax.rsqrt(ms + EPS) * g


def _sigmoid(x):
    return 1.0 / (1.0 + jnp.exp(-x))


def _kv_variants(k, v):
    rows = k.shape[0]
    lane = lax.broadcasted_iota(jnp.int32, (rows, LANES), 1)
    lo = lane < HEAD_DIM
    k_sw = pltpu.roll(k, HEAD_DIM, axis=1)
    v_sw = pltpu.roll(v, HEAD_DIM, axis=1)
    zero = jnp.zeros_like(k)
    one = jnp.ones_like(k)
    ones_lo = jnp.where(lo, one, zero).astype(jnp.bfloat16)
    ones_hi = jnp.where(lo, zero, one).astype(jnp.bfloat16)
    kz, vz = [], []
    for h in range(2):
        for half in range(2):
            src_k = k if h == half else k_sw
            src_v = v if h == half else v_sw
            keep = lo if half == 0 else jnp.logical_not(lo)
            kz.append(jnp.where(keep, src_k, zero).astype(jnp.bfloat16))
            vpart = jnp.where(keep, src_v, zero).astype(jnp.bfloat16)
            vz.append(jnp.concatenate([vpart, ones_lo if half == 0 else ones_hi], axis=1))
    return kz, vz


def _layer_kernel(*refs, T, CH, BLK, NS, N_TILES, KOUT, has_cache):
    it = iter(refs)
    sinks_ref = next(it)
    x_ref = next(it)
    p_ref = next(it)
    if has_cache:
        ck_ref = next(it)
        cv_ref = next(it)
        cs_ref = next(it)
    (norm_mix_ref, w_in_ref, conv_w_ref, conv_b_ref, ln_g_ref, ln_b_ref, attn_g_ref, conv_g_ref,
     w_out_ref, norm_ffn_ref, w_up_ref, w_down_ref, norm_ple_ref, w_gate_ref, w_ple_ref,
     final_norm_ref) = [next(it) for _ in range(16)]
    y_ref, nk_ref, nv_ref, nc_ref = [next(it) for _ in range(4)]
    (hn_ref, q_ref, kv_ref, kz_ref, vz_ref, ubuf_ref, e_ref, tt_ref, cbuf_ref, merged_ref, x1new_ref, hn2new_ref,
     hn2_ref, x1_ref, hact_ref) = [next(it) for _ in range(15)]

    t_idx = pl.program_id(0)
    s_idx = lax.rem(jnp.minimum(t_idx, N_TILES - 1), NS)
    cpb = BLK // CH
    win = WINDOW + CH
    grp = min(32, BLK)
    stride = grp // SUBLANES

    @pl.when(s_idx == 0)
    def _init_history():
        if has_cache:
            kz, vz = _kv_variants(ck_ref[0], cv_ref[0])
            for i in range(4):
                kz_ref[i, 0:WINDOW, :] = kz[i]
                vz_ref[i, 0:WINDOW, :] = vz[i]
            for sl in range(N_SLABS):
                ubuf_ref[sl, 0:CONV_HIST, :] = cs_ref[0, :, sl * LANES:(sl + 1) * LANES]
        else:
            for i in range(4):
                kz_ref[i, 0:WINDOW, :] = jnp.zeros((WINDOW, LANES), jnp.bfloat16)
                vz_ref[i, 0:WINDOW, :] = jnp.zeros((WINDOW, 2 * LANES), jnp.bfloat16)
            for sl in range(N_SLABS):
                ubuf_ref[sl, 0:CONV_HIST, :] = jnp.zeros((CONV_HIST, LANES), jnp.float32)

    @pl.when(t_idx == 0)
    def _init_handoff():
        x1_ref[...] = jnp.zeros(x1_ref.shape, x1_ref.dtype)
        hn2_ref[...] = jnp.zeros(hn2_ref.shape, hn2_ref.dtype)

    lane = lax.broadcasted_iota(jnp.int32, (2 * CH, LANES), 1)
    lo = lane < HEAD_DIM
    key_idx = lax.broadcasted_iota(jnp.int32, (2 * CH, win), 1)
    top_rows = lax.broadcasted_iota(jnp.int32, (2 * CH, 1), 0) < CH

    def _norm_to(dst_ref, src_ref_rows, g_ref, r0, n):
        for b0 in range(r0, r0 + n, BLK):
            dst_ref[b0:b0 + BLK, :] = _rms(src_ref_rows(b0), g_ref[...]).astype(jnp.bfloat16)

    def _project(r0, n):
        r = slice(r0, r0 + n)
        q_ref[r, :] = (jnp.dot(hn_ref[r, :], w_in_ref[:, Q_OFF:Q_OFF + ATTN_WIDTH],
                               preferred_element_type=jnp.float32) * (HEAD_DIM ** -0.5)).astype(jnp.bfloat16)
        kv_ref[r, :] = jnp.dot(hn_ref[r, :], w_in_ref[:, K_OFF:K_OFF + 2 * KV_WIDTH],
                               preferred_element_type=jnp.float32)
        ag = jnp.dot(hn_ref[r, :], w_in_ref[:, A_OFF:], preferred_element_type=jnp.float32)
        for sl in range(N_SLABS):
            a = ag[:, sl * LANES:(sl + 1) * LANES]
            g = ag[:, CONV_CH + sl * LANES:CONV_CH + (sl + 1) * LANES]
            ubuf_ref[sl, CONV_HIST + r0:CONV_HIST + r0 + n, :] = a * _sigmoid(g)
        for b0 in range(r0, r0 + n, BLK):
            kz, vz = _kv_variants(kv_ref[b0:b0 + BLK, 0:KV_WIDTH], kv_ref[b0:b0 + BLK, KV_WIDTH:])
            for j in range(4):
                kz_ref[j, WINDOW + b0:WINDOW + b0 + BLK, :] = kz[j]
                vz_ref[j, WINDOW + b0:WINDOW + b0 + BLK, :] = vz[j]

    def _scores(c):
        r0 = c * CH
        q_c = q_ref[r0:r0 + CH, :]
        if not has_cache:
            valid = key_idx >= WINDOW - (s_idx * T + r0)
        for h in range(2):
            lhs = jnp.concatenate([q_c[:, (2 * h) * LANES:(2 * h + 1) * LANES],
                                   q_c[:, (2 * h + 1) * LANES:(2 * h + 2) * LANES]], axis=0)
            sink_terms = []
            for half in range(2):
                s = lax.dot_general(lhs, kz_ref[2 * h + half, r0:r0 + win, :],
                                    (((1,), (1,)), ((), ())), preferred_element_type=jnp.float32)
                if not has_cache:
                    s = jnp.where(valid, s, -jnp.inf)
                sink = jnp.where(top_rows, sinks_ref[4 * h + half], sinks_ref[4 * h + 2 + half])
                m = jnp.maximum(jnp.max(s, axis=-1, keepdims=True), sink)
                e_ref[(c * 2 + h) * 2 + half] = jnp.exp(s - m).astype(jnp.bfloat16)
                sink_terms.append(jnp.exp(sink - m))
            tt_ref[c * 2 + h] = jnp.where(lo, sink_terms[0], sink_terms[1])

    def _conv(b0, sl):
        cols = slice(sl * LANES, (sl + 1) * LANES)
        starts = [b0 + g0 + v for g0 in range(0, BLK, grp) for v in range(stride)]
        acc = [jnp.zeros((SUBLANES, LANES), jnp.float32) for _ in starts]
        for j in range(CONV_KERNEL):
            w = conv_w_ref[j, :, cols]
            for n, st in enumerate(starts):
                acc[n] = acc[n] + w * ubuf_ref[sl, pl.ds(st + CONV_PAD + j, SUBLANES, stride=stride), :]
        bias = conv_b_ref[:, cols]
        for n, st in enumerate(starts):
            cbuf_ref[sl, pl.ds(st, SUBLANES, stride=stride), :] = acc[n] + bias

    def _pv(c):
        r0 = c * CH
        blocks = [None] * 4
        for h in range(2):
            acc = None
            for half in range(2):
                part = jnp.dot(e_ref[(c * 2 + h) * 2 + half], vz_ref[2 * h + half, r0:r0 + win, :],
                               preferred_element_type=jnp.float32)
                acc = part if acc is None else acc + part
            o = acc[:, :LANES] / (acc[:, LANES:] + tt_ref[c * 2 + h])
            blocks[2 * h] = o[:CH]
            blocks[2 * h + 1] = o[CH:]
        attn = jnp.concatenate(blocks, axis=1)
        merged_ref[r0:r0 + CH, :ATTN_WIDTH] = _rms(attn, attn_g_ref[...]).astype(jnp.bfloat16)

    def _ln(b0):
        r = slice(b0, b0 + BLK)
        cv = jnp.concatenate([cbuf_ref[sl, r, :] for sl in range(N_SLABS)], axis=1)
        mu = jnp.mean(cv, axis=-1, keepdims=True)
        xc = cv - mu
        y = xc * lax.rsqrt(jnp.mean(xc * xc, axis=-1, keepdims=True) + EPS) * ln_g_ref[...] + ln_b_ref[...]
        y = y * _sigmoid(y)
        merged_ref[r, ATTN_WIDTH:] = _rms(y, conv_g_ref[...]).astype(jnp.bfloat16)

    def _wout():
        x1new_ref[...] = x_ref[0] + jnp.dot(merged_ref[...], w_out_ref[...], preferred_element_type=jnp.float32)

    ff_group = D_FF // FF_SPLIT

    def _up(f):
        hid = jnp.dot(hn2_ref[...], w_up_ref[:, f * FF_BLOCK:(f + 1) * FF_BLOCK],
                      preferred_element_type=jnp.float32)
        hid = jnp.maximum(hid, 0.0)
        col = (f * FF_BLOCK) % ff_group
        hact_ref[:, col:col + FF_BLOCK] = (hid * hid).astype(jnp.bfloat16)

    def _down(fs):
        x1_ref[...] = x1_ref[...] + jnp.dot(hact_ref[...], w_down_ref[fs * ff_group:(fs + 1) * ff_group, :],
                                            preferred_element_type=jnp.float32)

    def _gate_ple():
        gate = _sigmoid(jnp.dot(hn2_ref[...], w_gate_ref[...], preferred_element_type=jnp.float32))
        ple = jnp.dot(p_ref[0].astype(jnp.bfloat16), w_ple_ref[...], preferred_element_type=jnp.float32)
        x1_ref[...] = x1_ref[...] + ple * gate

    def _norm_out():
        for b0 in range(0, T, BLK):
            y_ref[0, b0:b0 + BLK, :] = _rms(x1_ref[b0:b0 + BLK, :], final_norm_ref[...])

    mixer_units = [functools.partial(_scores, c) for c in range(T // CH)]
    for b0 in range(0, T, BLK):
        mixer_units += [functools.partial(_conv, b0, sl) for sl in range(N_SLABS)]
        mixer_units += [functools.partial(_pv, c) for c in range(b0 // CH, (b0 + BLK) // CH)]
        mixer_units.append(functools.partial(_ln, b0))
    per_group = ff_group // FF_BLOCK
    mlp_units = []
    for fs in range(FF_SPLIT):
        mlp_units += [functools.partial(_up, fs * per_group + f) for f in range(per_group)]
        mlp_units.append(functools.partial(_down, fs))

    _norm_to(hn_ref, lambda b0: x_ref[0, b0:b0 + BLK, :], norm_mix_ref, 0, T)
    _project(0, T)
    done = 0
    for k, mlp_unit in enumerate(mlp_units):
        upto = (k + 1) * len(mixer_units) // len(mlp_units)
        for unit in mixer_units[done:upto]:
            unit()
        done = upto
        mlp_unit()
    _wout()
    _norm_to(hn2_ref, lambda b0: x1_ref[b0:b0 + BLK, :], norm_ple_ref, 0, T)
    _norm_to(hn2new_ref, lambda b0: x1new_ref[b0:b0 + BLK, :], norm_ffn_ref, 0, T)
    _gate_ple()
    _norm_out()
    x1_ref[...] = x1new_ref[...]
    hn2_ref[...] = hn2new_ref[...]

    @pl.when(s_idx == NS - 1)
    def _emit_state():
        nk_ref[0] = kv_ref[T - KOUT:T, 0:KV_WIDTH]
        nv_ref[0] = kv_ref[T - KOUT:T, KV_WIDTH:]
        nc_ref[0] = jnp.concatenate([ubuf_ref[sl, T:T + CONV_HIST, :] for sl in range(N_SLABS)], axis=1)

    if NS > 1:
        for i in range(4):
            kz_ref[i, 0:WINDOW, :] = kz_ref[i, T:T + WINDOW, :]
            vz_ref[i, 0:WINDOW, :] = vz_ref[i, T:T + WINDOW, :]
        for sl in range(N_SLABS):
            ubuf_ref[sl, 0:CONV_HIST, :] = ubuf_ref[sl, T:T + CONV_HIST, :]


def _const_spec(shape):
    nd = len(shape)
    return pl.BlockSpec(shape, lambda t: (0,) * nd, pipeline_mode=pl.Buffered(1))


def _layer_call(x, p, cache, weights, sinks, *, T, CH, name):
    B, S, _ = x.shape
    NS = S // T
    BLK = min(128, T)
    KOUT = min(WINDOW, S)
    has_cache = cache is not None
    assert S % T == 0 and T % BLK == 0 and BLK % CH == 0 and KOUT <= T
    n_ch = T // CH
    win = WINDOW + CH
    n_tiles = B * NS

    def mixer_tile(t):
        return jnp.minimum(t, n_tiles - 1)

    def mlp_tile(t):
        return jnp.maximum(t - 1, 0)

    mixer_spec = lambda width: pl.BlockSpec((1, T, width), lambda t: (mixer_tile(t) // NS, mixer_tile(t) % NS, 0))
    mlp_spec = lambda width: pl.BlockSpec((1, T, width), lambda t: (mlp_tile(t) // NS, mlp_tile(t) % NS, 0))
    per_seq = lambda rows, width: pl.BlockSpec((1, rows, width), lambda t: (mixer_tile(t) // NS, 0, 0))

    in_specs = [pl.BlockSpec(memory_space=pltpu.SMEM), mixer_spec(D_MODEL), mlp_spec(PLE_DIM)]
    args = [sinks, x, p]
    if has_cache:
        in_specs += [per_seq(WINDOW, KV_WIDTH), per_seq(WINDOW, KV_WIDTH), per_seq(CONV_HIST, CONV_CH)]
        args += list(cache)
    in_specs += [_const_spec(w.shape) for w in weights]
    args += list(weights)

    out_shape = (
        jax.ShapeDtypeStruct((B, S, D_MODEL), jnp.float32),
        jax.ShapeDtypeStruct((B, KOUT, KV_WIDTH), jnp.float32),
        jax.ShapeDtypeStruct((B, KOUT, KV_WIDTH), jnp.float32),
        jax.ShapeDtypeStruct((B, CONV_HIST, CONV_CH), jnp.float32),
    )
    out_specs = (mlp_spec(D_MODEL), per_seq(KOUT, KV_WIDTH), per_seq(KOUT, KV_WIDTH),
                 per_seq(CONV_HIST, CONV_CH))
    scratch = [
        pltpu.VMEM((T, D_MODEL), jnp.bfloat16),
        pltpu.VMEM((T, ATTN_WIDTH), jnp.bfloat16),
        pltpu.VMEM((T, 2 * KV_WIDTH), jnp.float32),
        pltpu.VMEM((4, WINDOW + T, LANES), jnp.bfloat16),
        pltpu.VMEM((4, WINDOW + T, 2 * LANES), jnp.bfloat16),
        pltpu.VMEM((N_SLABS, CONV_HIST + T, LANES), jnp.float32),
        pltpu.VMEM((4 * n_ch, 2 * CH, win), jnp.bfloat16),
        pltpu.VMEM((2 * n_ch, 2 * CH, LANES), jnp.float32),
        pltpu.VMEM((N_SLABS, T, LANES), jnp.float32),
        pltpu.VMEM((T, D_MODEL), jnp.bfloat16),
        pltpu.VMEM((T, D_MODEL), jnp.float32),
        pltpu.VMEM((T, D_MODEL), jnp.bfloat16),
        pltpu.VMEM((T, D_MODEL), jnp.bfloat16),
        pltpu.VMEM((T, D_MODEL), jnp.float32),
        pltpu.VMEM((T, D_FF // FF_SPLIT), jnp.bfloat16),
    ]
    body = functools.partial(_layer_kernel, T=T, CH=CH, BLK=BLK, NS=NS, N_TILES=n_tiles, KOUT=KOUT,
                             has_cache=has_cache)
    return pl.pallas_call(
        body,
        grid=(n_tiles + 1,),
        in_specs=in_specs,
        out_specs=out_specs,
        out_shape=out_shape,
        scratch_shapes=scratch,
        compiler_params=pltpu.CompilerParams(
            dimension_semantics=("arbitrary",),
            vmem_limit_bytes=VMEM_LIMIT_BYTES),
        name=name,
    )(*args)


def _pick_tile(S, target):
    t = min(S, target)
    while S % t:
        t //= 2
    return t


def kernel(x_prompt, x_sample, p_prompt, p_sample, cache_k, cache_v, state_conv, norm_mix, w_in, sinks, conv_w, conv_b, ln_g, ln_b, attn_out_g, conv_out_g, w_out, norm_ffn, w_up, w_down, norm_ple, w_ple_gate, w_ple, final_norm):
    depth = w_in.shape[0]
    assert depth == 1, "single-layer step"
    bf = jnp.bfloat16
    row = lambda v: v.reshape(1, -1)
    conv_w8 = jnp.broadcast_to(conv_w[0][:, None, :], (CONV_KERNEL, SUBLANES, CONV_CH))
    weights = (row(norm_mix[0]), w_in[0].astype(bf), conv_w8, row(conv_b[0]), row(ln_g[0]), row(ln_b[0]),
               row(attn_out_g[0]), row(conv_out_g[0]), w_out[0].astype(bf), row(norm_ffn[0]),
               w_up[0].astype(bf), w_down[0].astype(bf), row(norm_ple[0]), w_ple_gate[0].astype(bf),
               w_ple[0].astype(bf), row(final_norm))
    sk = sinks[0]

    Bp, Sp, _ = x_prompt.shape
    yp, nkp, nvp, ncp = _layer_call(x_prompt, p_prompt[0], None, weights, sk,
                                    T=_pick_tile(Sp, 512), CH=64, name="prompt_layer")

    Bs, Ss, _ = x_sample.shape
    assert cache_k.shape[2] == WINDOW
    cache = (cache_k[0].reshape(Bs, WINDOW, KV_WIDTH), cache_v[0].reshape(Bs, WINDOW, KV_WIDTH),
             jnp.pad(state_conv[0], ((0, 0), (CONV_PAD, 0), (0, 0))))
    ys, nks, nvs, ncs = _layer_call(x_sample, p_sample[0], cache, weights, sk,
                                    T=Ss, CH=Ss, name="sample_layer")

    kv_shape = lambda a: a.reshape(1, a.shape[0], a.shape[1], 2, HEAD_DIM)
    return (yp, ys, kv_shape(nkp), kv_shape(nvp), ncp[None, :, CONV_PAD:, :],
            kv_shape(nks), kv_shape(nvs), ncs[None, :, CONV_PAD:, :])
```

```python
import functools

import jax
import jax.numpy as jnp
from jax import lax
from jax.experimental import pallas as pl
from jax.experimental.pallas import tpu as pltpu

D_MODEL = 1024
HEAD_DIM = 64
ATTN_WIDTH = 512
KV_WIDTH = 128
WINDOW = 128
CONV_CH = 512
CONV_KERNEL = 31
CONV_HIST = 32
CONV_PAD = CONV_HIST - (CONV_KERNEL - 1)
D_FF = 4096
PLE_DIM = 256
EPS = 1e-6
Q_OFF, K_OFF, A_OFF = 0, 512, 768
LANES = 128
SUBLANES = 8
N_SLABS = CONV_CH // LANES
FF_BLOCK = 1024
FF_SPLIT = 2
VMEM_LIMIT_BYTES = 56 * 1024 * 1024


def _rms(x, g):
    ms = jnp.mean(x * x, axis=-1, keepdims=True)
    return x * lax.rsqrt(ms + EPS) * g


def _sigmoid(x):
    return 1.0 / (1.0 + jnp.exp(-x))


def _kv_variants(k, v):
    rows = k.shape[0]
    lane = lax.broadcasted_iota(jnp.int32, (rows, LANES), 1)
    lo = lane < HEAD_DIM
    k_sw = pltpu.roll(k, HEAD_DIM, axis=1)
    v_sw = pltpu.roll(v, HEAD_DIM, axis=1)
    zero = jnp.zeros_like(k)
    one = jnp.ones_like(k)
    ones_lo = jnp.where(lo, one, zero).astype(jnp.bfloat16)
    ones_hi = jnp.where(lo, zero, one).astype(jnp.bfloat16)
    kz, vz = [], []
    for h in range(2):
        for half in range(2):
            src_k = k if h == half else k_sw
            src_v = v if h == half else v_sw
            keep = lo if half == 0 else jnp.logical_not(lo)
            kz.append(jnp.where(keep, src_k, zero).astype(jnp.bfloat16))
            vpart = jnp.where(keep, src_v, zero).astype(jnp.bfloat16)
            vz.append(jnp.concatenate([vpart, ones_lo if half == 0 else ones_hi], axis=1))
    return kz, vz


def _layer_kernel(*refs, T, L, CH, BLK, NS, N_TILES, KOUT, has_cache):
    it = iter(refs)
    sinks_ref = next(it)
    x_ref = next(it)
    p_ref = next(it)
    if has_cache:
        ck_ref = next(it)
        cv_ref = next(it)
        cs_ref = next(it)
    (norm_mix_ref, w_in_ref, conv_w_ref, conv_b_ref, ln_g_ref, ln_b_ref, attn_g_ref, conv_g_ref,
     w_out_ref, norm_ffn_ref, w_up_ref, w_down_ref, norm_ple_ref, w_gate_ref, w_ple_ref,
     final_norm_ref) = [next(it) for _ in range(16)]
    y_ref, nk_ref, nv_ref, nc_ref = [next(it) for _ in range(4)]
    (hn_ref, q_ref, kv_ref, kz_ref, vz_ref, ubuf_ref, e_ref, tt_ref, cbuf_ref, merged_ref, x1new_ref, hn2new_ref,
     hn2_ref, x1_ref, hact_ref) = [next(it) for _ in range(15)]

    t_idx = pl.program_id(0)
    s_idx = lax.rem(jnp.minimum(t_idx, N_TILES - 1), NS)
    win = WINDOW + CH
    grp = min(32, BLK, L)
    stride = grp // SUBLANES
    n_seq = T // L
    kv_rows = WINDOW + L
    conv_rows = CONV_HIST + L

    @pl.when(s_idx == 0)
    def _init_history():
        for q in range(n_seq):
            kb, cb = q * kv_rows, q * conv_rows
            if has_cache:
                kz, vz = _kv_variants(ck_ref[q], cv_ref[q])
                for i in range(4):
                    kz_ref[i, kb:kb + WINDOW, :] = kz[i]
                    vz_ref[i, kb:kb + WINDOW, :] = vz[i]
                for sl in range(N_SLABS):
                    ubuf_ref[sl, cb:cb + CONV_HIST, :] = cs_ref[q, :, sl * LANES:(sl + 1) * LANES]
            else:
                for i in range(4):
                    kz_ref[i, kb:kb + WINDOW, :] = jnp.zeros((WINDOW, LANES), jnp.bfloat16)
                    vz_ref[i, kb:kb + WINDOW, :] = jnp.zeros((WINDOW, 2 * LANES), jnp.bfloat16)
                for sl in range(N_SLABS):
                    ubuf_ref[sl, cb:cb + CONV_HIST, :] = jnp.zeros((CONV_HIST, LANES), jnp.float32)

    @pl.when(t_idx == 0)
    def _init_handoff():
        x1_ref[...] = jnp.zeros(x1_ref.shape, x1_ref.dtype)
        hn2_ref[...] = jnp.zeros(hn2_ref.shape, hn2_ref.dtype)

    lane = lax.broadcasted_iota(jnp.int32, (2 * CH, LANES), 1)
    lo = lane < HEAD_DIM
    key_idx = lax.broadcasted_iota(jnp.int32, (2 * CH, win), 1)
    top_rows = lax.broadcasted_iota(jnp.int32, (2 * CH, 1), 0) < CH

    def _norm_to(dst_ref, src_ref_rows, g_ref, r0, n):
        for b0 in range(r0, r0 + n, BLK):
            dst_ref[b0:b0 + BLK, :] = _rms(src_ref_rows(b0), g_ref[...]).astype(jnp.bfloat16)

    def _project(r0, n):
        r = slice(r0, r0 + n)
        q_ref[r, :] = (jnp.dot(hn_ref[r, :], w_in_ref[:, Q_OFF:Q_OFF + ATTN_WIDTH],
                               preferred_element_type=jnp.float32) * (HEAD_DIM ** -0.5)).astype(jnp.bfloat16)
        kv_ref[r, :] = jnp.dot(hn_ref[r, :], w_in_ref[:, K_OFF:K_OFF + 2 * KV_WIDTH],
                               preferred_element_type=jnp.float32)
        ag = jnp.dot(hn_ref[r, :], w_in_ref[:, A_OFF:], preferred_element_type=jnp.float32)
        for sl in range(N_SLABS):
            a = ag[:, sl * LANES:(sl + 1) * LANES]
            g = ag[:, CONV_CH + sl * LANES:CONV_CH + (sl + 1) * LANES]
            u = a * _sigmoid(g)
            for q in range(n_seq):
                cb = q * conv_rows + CONV_HIST
                ubuf_ref[sl, cb:cb + L, :] = u[q * L:(q + 1) * L]
        rows = min(BLK, L)
        for b0 in range(r0, r0 + n, rows):
            kz, vz = _kv_variants(kv_ref[b0:b0 + rows, 0:KV_WIDTH], kv_ref[b0:b0 + rows, KV_WIDTH:])
            kb = (b0 // L) * kv_rows + WINDOW + b0 % L
            for j in range(4):
                kz_ref[j, kb:kb + rows, :] = kz[j]
                vz_ref[j, kb:kb + rows, :] = vz[j]

    def _scores(c):
        r0 = c * CH
        w0 = (r0 // L) * kv_rows + r0 % L
        q_c = q_ref[r0:r0 + CH, :]
        if not has_cache:
            valid = key_idx >= WINDOW - (s_idx * T + r0)
        for h in range(2):
            lhs = jnp.concatenate([q_c[:, (2 * h) * LANES:(2 * h + 1) * LANES],
                                   q_c[:, (2 * h + 1) * LANES:(2 * h + 2) * LANES]], axis=0)
            sink_terms = []
            for half in range(2):
                s = lax.dot_general(lhs, kz_ref[2 * h + half, w0:w0 + win, :],
                                    (((1,), (1,)), ((), ())), preferred_element_type=jnp.float32)
                if not has_cache:
                    s = jnp.where(valid, s, -jnp.inf)
                sink = jnp.where(top_rows, sinks_ref[4 * h + half], sinks_ref[4 * h + 2 + half])
                m = jnp.maximum(jnp.max(s, axis=-1, keepdims=True), sink)
                e_ref[(c * 2 + h) * 2 + half] = jnp.exp(s - m).astype(jnp.bfloat16)
                sink_terms.append(jnp.exp(sink - m))
            tt_ref[c * 2 + h] = jnp.where(lo, sink_terms[0], sink_terms[1])

    def _conv(b0, sl):
        cols = slice(sl * LANES, (sl + 1) * LANES)
        starts = [b0 + g0 + v for g0 in range(0, BLK, grp) for v in range(stride)]
        acc = [jnp.zeros((SUBLANES, LANES), jnp.float32) for _ in starts]
        srcs = [(st // L) * conv_rows + st % L + CONV_PAD for st in starts]
        for j in range(CONV_KERNEL):
            w = conv_w_ref[j, :, cols]
            for n, src in enumerate(srcs):
                acc[n] = acc[n] + w * ubuf_ref[sl, pl.ds(src + j, SUBLANES, stride=stride), :]
        bias = conv_b_ref[:, cols]
        for n, st in enumerate(starts):
            cbuf_ref[sl, pl.ds(st, SUBLANES, stride=stride), :] = acc[n] + bias

    def _pv(c):
        r0 = c * CH
        w0 = (r0 // L) * kv_rows + r0 % L
        blocks = [None] * 4
        for h in range(2):
            acc = None
            for half in range(2):
                part = jnp.dot(e_ref[(c * 2 + h) * 2 + half], vz_ref[2 * h + half, w0:w0 + win, :],
                               preferred_element_type=jnp.float32)
                acc = part if acc is None else acc + part
            o = acc[:, :LANES] / (acc[:, LANES:] + tt_ref[c * 2 + h])
            blocks[2 * h] = o[:CH]
            blocks[2 * h + 1] = o[CH:]
        attn = jnp.concatenate(blocks, axis=1)
        merged_ref[r0:r0 + CH, :ATTN_WIDTH] = _rms(attn, attn_g_ref[...]).astype(jnp.bfloat16)

    def _ln(b0):
        r = slice(b0, b0 + BLK)
        cv = jnp.concatenate([cbuf_ref[sl, r, :] for sl in range(N_SLABS)], axis=1)
        mu = jnp.mean(cv, axis=-1, keepdims=True)
        xc = cv - mu
        y = xc * lax.rsqrt(jnp.mean(xc * xc, axis=-1, keepdims=True) + EPS) * ln_g_ref[...] + ln_b_ref[...]
        y = y * _sigmoid(y)
        merged_ref[r, ATTN_WIDTH:] = _rms(y, conv_g_ref[...]).astype(jnp.bfloat16)

    def _wout():
        x1new_ref[...] = x_ref[0] + jnp.dot(merged_ref[...], w_out_ref[...], preferred_element_type=jnp.float32)

    ff_group = D_FF // FF_SPLIT

    def _up(f):
        hid = jnp.dot(hn2_ref[...], w_up_ref[:, f * FF_BLOCK:(f + 1) * FF_BLOCK],
                      preferred_element_type=jnp.float32)
        hid = jnp.maximum(hid, 0.0)
        col = (f * FF_BLOCK) % ff_group
        hact_ref[:, col:col + FF_BLOCK] = (hid * hid).astype(jnp.bfloat16)

    def _down(fs):
        x1_ref[...] = x1_ref[...] + jnp.dot(hact_ref[...], w_down_ref[fs * ff_group:(fs + 1) * ff_group, :],
                                            preferred_element_type=jnp.float32)

    def _gate_ple():
        gate = _sigmoid(jnp.dot(hn2_ref[...], w_gate_ref[...], preferred_element_type=jnp.float32))
        ple = jnp.dot(p_ref[0].astype(jnp.bfloat16), w_ple_ref[...], preferred_element_type=jnp.float32)
        x1_ref[...] = x1_ref[...] + ple * gate

    def _norm_out():
        for b0 in range(0, T, BLK):
            y_ref[0, b0:b0 + BLK, :] = _rms(x1_ref[b0:b0 + BLK, :], final_norm_ref[...])

    mixer_units = [functools.partial(_scores, c) for c in range(T // CH)]
    for b0 in range(0, T, BLK):
        mixer_units += [functools.partial(_conv, b0, sl) for sl in range(N_SLABS)]
        mixer_units += [functools.partial(_pv, c) for c in range(b0 // CH, (b0 + BLK) // CH)]
        mixer_units.append(functools.partial(_ln, b0))
    per_group = ff_group // FF_BLOCK
    mlp_units = []
    for fs in range(FF_SPLIT):
        mlp_units += [functools.partial(_up, fs * per_group + f) for f in range(per_group)]
        mlp_units.append(functools.partial(_down, fs))

    mlp_units[0]()
    _norm_to(hn_ref, lambda b0: x_ref[0, b0:b0 + BLK, :], norm_mix_ref, 0, T)
    _project(0, T)
    done = 0
    for k, mlp_unit in enumerate(mlp_units[1:]):
        upto = (k + 1) * len(mixer_units) // (len(mlp_units) - 1)
        for unit in mixer_units[done:upto]:
            unit()
        done = upto
        mlp_unit()
    _wout()
    _norm_to(hn2_ref, lambda b0: x1_ref[b0:b0 + BLK, :], norm_ple_ref, 0, T)
    _norm_to(hn2new_ref, lambda b0: x1new_ref[b0:b0 + BLK, :], norm_ffn_ref, 0, T)
    _gate_ple()
    _norm_out()
    x1_ref[...] = x1new_ref[...]
    hn2_ref[...] = hn2new_ref[...]

    @pl.when(s_idx == NS - 1)
    def _emit_state():
        for q in range(n_seq):
            last = slice((q + 1) * L - KOUT, (q + 1) * L)
            nk_ref[q] = kv_ref[last, 0:KV_WIDTH]
            nv_ref[q] = kv_ref[last, KV_WIDTH:]
            cb = q * conv_rows + L
            nc_ref[q] = jnp.concatenate([ubuf_ref[sl, cb:cb + CONV_HIST, :] for sl in range(N_SLABS)], axis=1)

    if NS > 1:
        for i in range(4):
            kz_ref[i, 0:WINDOW, :] = kz_ref[i, T:T + WINDOW, :]
            vz_ref[i, 0:WINDOW, :] = vz_ref[i, T:T + WINDOW, :]
        for sl in range(N_SLABS):
            ubuf_ref[sl, 0:CONV_HIST, :] = ubuf_ref[sl, T:T + CONV_HIST, :]


def _const_spec(shape):
    nd = len(shape)
    return pl.BlockSpec(shape, lambda t: (0,) * nd, pipeline_mode=pl.Buffered(1))


def _layer_call(x, p, cache, weights, sinks, *, L, n_seq, CH, name):
    B, S, _ = x.shape
    T = n_seq * L
    NS = S // L
    BLK = min(128, T)
    KOUT = min(WINDOW, S)
    has_cache = cache is not None
    assert S % L == 0 and B % n_seq == 0 and (n_seq == 1 or NS == 1)
    assert T % BLK == 0 and BLK % CH == 0 and L % CH == 0 and (L % BLK == 0 or BLK % L == 0) and KOUT <= L
    n_ch = T // CH
    win = WINDOW + CH
    n_tiles = (B // n_seq) * NS
    x = x.reshape(B // n_seq, n_seq * S, D_MODEL)
    p = p.reshape(B // n_seq, n_seq * S, PLE_DIM)

    def mixer_tile(t):
        return jnp.minimum(t, n_tiles - 1)

    def mlp_tile(t):
        return jnp.maximum(t - 1, 0)

    mixer_spec = lambda width: pl.BlockSpec((1, T, width), lambda t: (mixer_tile(t) // NS, mixer_tile(t) % NS, 0))
    mlp_spec = lambda width: pl.BlockSpec((1, T, width), lambda t: (mlp_tile(t) // NS, mlp_tile(t) % NS, 0))
    per_seq = lambda rows, width: pl.BlockSpec((n_seq, rows, width), lambda t: (mixer_tile(t) // NS, 0, 0))

    in_specs = [pl.BlockSpec(memory_space=pltpu.SMEM), mixer_spec(D_MODEL), mlp_spec(PLE_DIM)]
    args = [sinks, x, p]
    if has_cache:
        in_specs += [per_seq(WINDOW, KV_WIDTH), per_seq(WINDOW, KV_WIDTH), per_seq(CONV_HIST, CONV_CH)]
        args += list(cache)
    in_specs += [_const_spec(w.shape) for w in weights]
    args += list(weights)

    out_shape = (
        jax.ShapeDtypeStruct(x.shape, jnp.float32),
        jax.ShapeDtypeStruct((B, KOUT, KV_WIDTH), jnp.float32),
        jax.ShapeDtypeStruct((B, KOUT, KV_WIDTH), jnp.float32),
        jax.ShapeDtypeStruct((B, CONV_HIST, CONV_CH), jnp.float32),
    )
    out_specs = (mlp_spec(D_MODEL), per_seq(KOUT, KV_WIDTH), per_seq(KOUT, KV_WIDTH),
                 per_seq(CONV_HIST, CONV_CH))
    scratch = [
        pltpu.VMEM((T, D_MODEL), jnp.bfloat16),
        pltpu.VMEM((T, ATTN_WIDTH), jnp.bfloat16),
        pltpu.VMEM((T, 2 * KV_WIDTH), jnp.float32),
        pltpu.VMEM((4, n_seq * (WINDOW + L), LANES), jnp.bfloat16),
        pltpu.VMEM((4, n_seq * (WINDOW + L), 2 * LANES), jnp.bfloat16),
        pltpu.VMEM((N_SLABS, n_seq * (CONV_HIST + L), LANES), jnp.float32),
        pltpu.VMEM((4 * n_ch, 2 * CH, win), jnp.bfloat16),
        pltpu.VMEM((2 * n_ch, 2 * CH, LANES), jnp.float32),
        pltpu.VMEM((N_SLABS, T, LANES), jnp.float32),
        pltpu.VMEM((T, D_MODEL), jnp.bfloat16),
        pltpu.VMEM((T, D_MODEL), jnp.float32),
        pltpu.VMEM((T, D_MODEL), jnp.bfloat16),
        pltpu.VMEM((T, D_MODEL), jnp.bfloat16),
        pltpu.VMEM((T, D_MODEL), jnp.float32),
        pltpu.VMEM((T, D_FF // FF_SPLIT), jnp.bfloat16),
    ]
    body = functools.partial(_layer_kernel, T=T, L=L, CH=CH, BLK=BLK, NS=NS, N_TILES=n_tiles, KOUT=KOUT,
                             has_cache=has_cache)
    y, nk, nv, nc = pl.pallas_call(
        body,
        grid=(n_tiles + 1,),
        in_specs=in_specs,
        out_specs=out_specs,
        out_shape=out_shape,
        scratch_shapes=scratch,
        compiler_params=pltpu.CompilerParams(
            dimension_semantics=("arbitrary",),
            vmem_limit_bytes=VMEM_LIMIT_BYTES),
        name=name,
    )(*args)
    return y.reshape(B, S, D_MODEL), nk, nv, nc


def _pick_tile(S, target):
    t = min(S, target)
    while S % t:
        t //= 2
    return t


def kernel(x_prompt, x_sample, p_prompt, p_sample, cache_k, cache_v, state_conv, norm_mix, w_in, sinks, conv_w, conv_b, ln_g, ln_b, attn_out_g, conv_out_g, w_out, norm_ffn, w_up, w_down, norm_ple, w_ple_gate, w_ple, final_norm):
    depth = w_in.shape[0]
    assert depth == 1, "single-layer step"
    row = lambda v: v.reshape(1, -1)
    conv_w8 = jnp.broadcast_to(conv_w[0][:, None, :], (CONV_KERNEL, SUBLANES, CONV_CH))
    bf = jnp.bfloat16
    weights = (row(norm_mix[0]), w_in[0].astype(bf), conv_w8, row(conv_b[0]), row(ln_g[0]), row(ln_b[0]),
               row(attn_out_g[0]), row(conv_out_g[0]), w_out[0].astype(bf), row(norm_ffn[0]),
               w_up[0].astype(bf), w_down[0].astype(bf), row(norm_ple[0]), w_ple_gate[0].astype(bf),
               w_ple[0].astype(bf), row(final_norm))
    sk = sinks[0]

    Bp, Sp, _ = x_prompt.shape
    yp, nkp, nvp, ncp = _layer_call(x_prompt, p_prompt[0], None, weights, sk,
                                    L=_pick_tile(Sp, 512), n_seq=1, CH=64, name="prompt_layer")

    Bs, Ss, _ = x_sample.shape
    assert cache_k.shape[2] == WINDOW
    cache = (cache_k[0].reshape(Bs, WINDOW, KV_WIDTH), cache_v[0].reshape(Bs, WINDOW, KV_WIDTH),
             jnp.pad(state_conv[0], ((0, 0), (CONV_PAD, 0), (0, 0))))
    ys, nks, nvs, ncs = _layer_call(x_sample, p_sample[0], cache, weights, sk,
                                    L=Ss, n_seq=Bs, CH=Ss, name="sample_layer")

    kv_shape = lambda a: a.reshape(1, a.shape[0], a.shape[1], 2, HEAD_DIM)
    return (yp, ys, kv_shape(nkp), kv_shape(nvp), ncp[None, :, CONV_PAD:, :],
            kv_shape(nks), kv_shape(nvs), ncs[None, :, CONV_PAD:, :])
```

```python
import functools

import jax
import jax.numpy as jnp
from jax import lax
from jax.experimental import pallas as pl
from jax.experimental.pallas import tpu as pltpu

D_MODEL = 1024
HEAD_DIM = 64
ATTN_WIDTH = 512
KV_WIDTH = 128
WINDOW = 128
CONV_CH = 512
CONV_KERNEL = 31
CONV_HIST = 32
CONV_PAD = CONV_HIST - (CONV_KERNEL - 1)
D_FF = 4096
PLE_DIM = 256
EPS = 1e-6
LOG2E = 1.4426950408889634
Q_OFF, K_OFF, A_OFF = 0, 512, 768
LANES = 128
SUBLANES = 8
N_SLABS = CONV_CH // LANES
FF_BLOCK = 1024
FF_SPLIT = 2
VMEM_LIMIT_BYTES = 56 * 1024 * 1024


def _rms(x, g):
    ms = jnp.mean(x * x, axis=-1, keepdims=True)
    return x * lax.rsqrt(ms + EPS) * g


def _sigmoid(x):
    return 1.0 / (1.0 + jnp.exp(-x))


def _kv_variants(k, v):
    rows = k.shape[0]
    lane = lax.broadcasted_iota(jnp.int32, (rows, LANES), 1)
    lo = lane < HEAD_DIM
    k_sw = pltpu.roll(k, HEAD_DIM, axis=1)
    v_sw = pltpu.roll(v, HEAD_DIM, axis=1)
    zero = jnp.zeros_like(k)
    one = jnp.ones_like(k)
    ones_lo = jnp.where(lo, one, zero).astype(jnp.bfloat16)
    ones_hi = jnp.where(lo, zero, one).astype(jnp.bfloat16)
    kz, vz = [], []
    for h in range(2):
        for half in range(2):
            src_k = k if h == half else k_sw
            src_v = v if h == half else v_sw
            keep = lo if half == 0 else jnp.logical_not(lo)
            kz.append(jnp.where(keep, src_k, zero).astype(jnp.bfloat16))
            vpart = jnp.where(keep, src_v, zero).astype(jnp.bfloat16)
            vz.append(jnp.concatenate([vpart, ones_lo if half == 0 else ones_hi], axis=1))
    return kz, vz


def _layer_kernel(*refs, T, L, CH, BLK, NS, N_TILES, KOUT, has_cache):
    it = iter(refs)
    sinks_ref = next(it)
    x_ref = next(it)
    p_ref = next(it)
    if has_cache:
        ck_ref = next(it)
        cv_ref = next(it)
        cs_ref = next(it)
    (norm_mix_ref, w_in_ref, conv_w_ref, conv_b_ref, ln_g_ref, ln_b_ref, attn_g_ref, conv_g_ref,
     w_out_ref, norm_ffn_ref, w_up_ref, w_down_ref, norm_ple_ref, w_gate_ref, w_ple_ref,
     final_norm_ref) = [next(it) for _ in range(16)]
    y_ref, nk_ref, nv_ref, nc_ref = [next(it) for _ in range(4)]
    (hn_ref, q_ref, kv_ref, kz_ref, vz_ref, ubuf_ref, e_ref, tt_ref, cbuf_ref, merged_ref, x1new_ref, hn2new_ref,
     hn2_ref, x1_ref, hact_ref) = [next(it) for _ in range(15)]

    t_idx = pl.program_id(0)
    s_idx = lax.rem(jnp.minimum(t_idx, N_TILES - 1), NS)
    win = WINDOW + CH
    grp = min(32, BLK, L)
    stride = grp // SUBLANES
    n_seq = T // L
    kv_rows = WINDOW + L
    conv_rows = CONV_HIST + L

    @pl.when(s_idx == 0)
    def _init_history():
        for q in range(n_seq):
            kb, cb = q * kv_rows, q * conv_rows
            if has_cache:
                kz, vz = _kv_variants(ck_ref[q], cv_ref[q])
                for i in range(4):
                    kz_ref[i, kb:kb + WINDOW, :] = kz[i]
                    vz_ref[i, kb:kb + WINDOW, :] = vz[i]
                for sl in range(N_SLABS):
                    ubuf_ref[sl, cb:cb + CONV_HIST, :] = cs_ref[q, :, sl * LANES:(sl + 1) * LANES]
            else:
                for i in range(4):
                    kz_ref[i, kb:kb + WINDOW, :] = jnp.zeros((WINDOW, LANES), jnp.bfloat16)
                    vz_ref[i, kb:kb + WINDOW, :] = jnp.zeros((WINDOW, 2 * LANES), jnp.bfloat16)
                for sl in range(N_SLABS):
                    ubuf_ref[sl, cb:cb + CONV_HIST, :] = jnp.zeros((CONV_HIST, LANES), jnp.float32)

    @pl.when(t_idx == 0)
    def _init_handoff():
        x1_ref[...] = jnp.zeros(x1_ref.shape, x1_ref.dtype)
        hn2_ref[...] = jnp.zeros(hn2_ref.shape, hn2_ref.dtype)

    lane = lax.broadcasted_iota(jnp.int32, (2 * CH, LANES), 1)
    lo = lane < HEAD_DIM
    key_idx = lax.broadcasted_iota(jnp.int32, (2 * CH, win), 1)
    top_rows = lax.broadcasted_iota(jnp.int32, (2 * CH, 1), 0) < CH

    def _norm_to(dst_ref, src_ref_rows, g_ref, r0, n):
        for b0 in range(r0, r0 + n, BLK):
            dst_ref[b0:b0 + BLK, :] = _rms(src_ref_rows(b0), g_ref[...]).astype(jnp.bfloat16)

    def _project(r0, n):
        r = slice(r0, r0 + n)
        q_ref[r, :] = (jnp.dot(hn_ref[r, :], w_in_ref[:, Q_OFF:Q_OFF + ATTN_WIDTH],
                               preferred_element_type=jnp.float32) * (HEAD_DIM ** -0.5 * LOG2E)).astype(jnp.bfloat16)
        kv_ref[r, :] = jnp.dot(hn_ref[r, :], w_in_ref[:, K_OFF:K_OFF + 2 * KV_WIDTH],
                               preferred_element_type=jnp.float32)
        ag = jnp.dot(hn_ref[r, :], w_in_ref[:, A_OFF:], preferred_element_type=jnp.float32)
        for sl in range(N_SLABS):
            a = ag[:, sl * LANES:(sl + 1) * LANES]
            g = ag[:, CONV_CH + sl * LANES:CONV_CH + (sl + 1) * LANES]
            u = a * _sigmoid(g)
            for q in range(n_seq):
                cb = q * conv_rows + CONV_HIST
                ubuf_ref[sl, cb:cb + L, :] = u[q * L:(q + 1) * L]
        rows = min(BLK, L)
        for b0 in range(r0, r0 + n, rows):
            kz, vz = _kv_variants(kv_ref[b0:b0 + rows, 0:KV_WIDTH], kv_ref[b0:b0 + rows, KV_WIDTH:])
            kb = (b0 // L) * kv_rows + WINDOW + b0 % L
            for j in range(4):
                kz_ref[j, kb:kb + rows, :] = kz[j]
                vz_ref[j, kb:kb + rows, :] = vz[j]

    def _scores(c):
        r0 = c * CH
        w0 = (r0 // L) * kv_rows + r0 % L
        q_c = q_ref[r0:r0 + CH, :]
        masked = not has_cache and r0 < WINDOW
        if masked:
            valid = key_idx >= WINDOW - (s_idx * T + r0)
        for h in range(2):
            lhs = jnp.concatenate([q_c[:, (2 * h) * LANES:(2 * h + 1) * LANES],
                                   q_c[:, (2 * h + 1) * LANES:(2 * h + 2) * LANES]], axis=0)
            sink_terms = []
            for half in range(2):
                s = lax.dot_general(lhs, kz_ref[2 * h + half, w0:w0 + win, :],
                                    (((1,), (1,)), ((), ())), preferred_element_type=jnp.float32)
                if masked:
                    s = jnp.where(valid, s, -jnp.inf)
                sink = jnp.where(top_rows, sinks_ref[4 * h + half] * LOG2E, sinks_ref[4 * h + 2 + half] * LOG2E)
                m = jnp.maximum(jnp.max(s, axis=-1, keepdims=True), sink)
                e_ref[(c * 2 + h) * 2 + half] = jnp.exp2(s - m).astype(jnp.bfloat16)
                sink_terms.append(jnp.exp2(sink - m))
            tt_ref[c * 2 + h] = jnp.where(lo, sink_terms[0], sink_terms[1])

    def _conv(b0, sl):
        cols = slice(sl * LANES, (sl + 1) * LANES)
        starts = [b0 + g0 + v for g0 in range(0, BLK, grp) for v in range(stride)]
        acc = [jnp.zeros((SUBLANES, LANES), jnp.float32) for _ in starts]
        srcs = [(st // L) * conv_rows + st % L + CONV_PAD for st in starts]
        for j in range(CONV_KERNEL):
            w = conv_w_ref[j, :, cols]
            for n, src in enumerate(srcs):
                acc[n] = acc[n] + w * ubuf_ref[sl, pl.ds(src + j, SUBLANES, stride=stride), :]
        bias = conv_b_ref[:, cols]
        for n, st in enumerate(starts):
            cbuf_ref[sl, pl.ds(st, SUBLANES, stride=stride), :] = acc[n] + bias

    def _pv(c):
        r0 = c * CH
        w0 = (r0 // L) * kv_rows + r0 % L
        blocks = [None] * 4
        for h in range(2):
            acc = None
            for half in range(2):
                part = jnp.dot(e_ref[(c * 2 + h) * 2 + half], vz_ref[2 * h + half, w0:w0 + win, :],
                               preferred_element_type=jnp.float32)
                acc = part if acc is None else acc + part
            o = acc[:, :LANES] / (acc[:, LANES:] + tt_ref[c * 2 + h])
            blocks[2 * h] = o[:CH]
            blocks[2 * h + 1] = o[CH:]
        attn = jnp.concatenate(blocks, axis=1)
        merged_ref[r0:r0 + CH, :ATTN_WIDTH] = _rms(attn, attn_g_ref[...]).astype(jnp.bfloat16)

    def _ln(b0):
        r = slice(b0, b0 + BLK)
        cv = jnp.concatenate([cbuf_ref[sl, r, :] for sl in range(N_SLABS)], axis=1)
        mu = jnp.mean(cv, axis=-1, keepdims=True)
        xc = cv - mu
        y = xc * lax.rsqrt(jnp.mean(xc * xc, axis=-1, keepdims=True) + EPS) * ln_g_ref[...] + ln_b_ref[...]
        y = y * _sigmoid(y)
        merged_ref[r, ATTN_WIDTH:] = _rms(y, conv_g_ref[...]).astype(jnp.bfloat16)

    def _wout():
        x1new_ref[...] = x_ref[0] + jnp.dot(merged_ref[...], w_out_ref[...], preferred_element_type=jnp.float32)

    ff_group = D_FF // FF_SPLIT

    def _up(f):
        hid = jnp.dot(hn2_ref[...], w_up_ref[:, f * FF_BLOCK:(f + 1) * FF_BLOCK],
                      preferred_element_type=jnp.float32)
        hid = jnp.maximum(hid.astype(jnp.bfloat16), 0.0)
        col = (f * FF_BLOCK) % ff_group
        hact_ref[:, col:col + FF_BLOCK] = hid * hid

    def _down(fs):
        x1_ref[...] = x1_ref[...] + jnp.dot(hact_ref[...], w_down_ref[fs * ff_group:(fs + 1) * ff_group, :],
                                            preferred_element_type=jnp.float32)

    def _gate_ple():
        gate = _sigmoid(jnp.dot(hn2_ref[...], w_gate_ref[...], preferred_element_type=jnp.float32))
        ple = jnp.dot(p_ref[0].astype(jnp.bfloat16), w_ple_ref[...], preferred_element_type=jnp.float32)
        x1_ref[...] = x1_ref[...] + ple * gate

    def _norm_out():
        for b0 in range(0, T, BLK):
            y_ref[0, b0:b0 + BLK, :] = _rms(x1_ref[b0:b0 + BLK, :], final_norm_ref[...])

    mixer_units = [functools.partial(_scores, c) for c in range(T // CH)]
    for b0 in range(0, T, BLK):
        mixer_units += [functools.partial(_conv, b0, sl) for sl in range(N_SLABS)]
        mixer_units += [functools.partial(_pv, c) for c in range(b0 // CH, (b0 + BLK) // CH)]
        mixer_units.append(functools.partial(_ln, b0))
    per_group = ff_group // FF_BLOCK
    mlp_units = []
    for fs in range(FF_SPLIT):
        mlp_units += [functools.partial(_up, fs * per_group + f) for f in range(per_group)]
        mlp_units.append(functools.partial(_down, fs))

    mlp_units[0]()
    _norm_to(hn_ref, lambda b0: x_ref[0, b0:b0 + BLK, :], norm_mix_ref, 0, T)
    _project(0, T)
    done = 0
    for k, mlp_unit in enumerate(mlp_units[1:]):
        upto = (k + 1) * len(mixer_units) // (len(mlp_units) - 1)
        for unit in mixer_units[done:upto]:
            unit()
        done = upto
        mlp_unit()
    _wout()
    _norm_to(hn2_ref, lambda b0: x1_ref[b0:b0 + BLK, :], norm_ple_ref, 0, T)
    _norm_to(hn2new_ref, lambda b0: x1new_ref[b0:b0 + BLK, :], norm_ffn_ref, 0, T)
    _gate_ple()
    _norm_out()
    x1_ref[...] = x1new_ref[...]
    hn2_ref[...] = hn2new_ref[...]

    @pl.when(s_idx == NS - 1)
    def _emit_state():
        for q in range(n_seq):
            last = slice((q + 1) * L - KOUT, (q + 1) * L)
            nk_ref[q] = kv_ref[last, 0:KV_WIDTH]
            nv_ref[q] = kv_ref[last, KV_WIDTH:]
            cb = q * conv_rows + L
            nc_ref[q] = jnp.concatenate([ubuf_ref[sl, cb:cb + CONV_HIST, :] for sl in range(N_SLABS)], axis=1)

    if NS > 1:
        for i in range(4):
            kz_ref[i, 0:WINDOW, :] = kz_ref[i, T:T + WINDOW, :]
            vz_ref[i, 0:WINDOW, :] = vz_ref[i, T:T + WINDOW, :]
        for sl in range(N_SLABS):
            ubuf_ref[sl, 0:CONV_HIST, :] = ubuf_ref[sl, T:T + CONV_HIST, :]


def _const_spec(shape):
    nd = len(shape)
    return pl.BlockSpec(shape, lambda t: (0,) * nd, pipeline_mode=pl.Buffered(1))


def _layer_call(x, p, cache, weights, sinks, *, L, n_seq, CH, name):
    B, S, _ = x.shape
    T = n_seq * L
    NS = S // L
    BLK = min(128, T)
    KOUT = min(WINDOW, S)
    has_cache = cache is not None
    assert S % L == 0 and B % n_seq == 0 and (n_seq == 1 or (NS == 1 and has_cache))
    assert T % BLK == 0 and BLK % CH == 0 and L % CH == 0 and (L % BLK == 0 or BLK % L == 0) and KOUT <= L
    n_ch = T // CH
    win = WINDOW + CH
    n_tiles = (B // n_seq) * NS
    x = x.reshape(B // n_seq, n_seq * S, D_MODEL)
    p = p.reshape(B // n_seq, n_seq * S, PLE_DIM)

    def mixer_tile(t):
        return jnp.minimum(t, n_tiles - 1)

    def mlp_tile(t):
        return jnp.maximum(t - 1, 0)

    mixer_spec = lambda width: pl.BlockSpec((1, T, width), lambda t: (mixer_tile(t) // NS, mixer_tile(t) % NS, 0))
    mlp_spec = lambda width: pl.BlockSpec((1, T, width), lambda t: (mlp_tile(t) // NS, mlp_tile(t) % NS, 0))
    per_seq = lambda rows, width: pl.BlockSpec((n_seq, rows, width), lambda t: (mixer_tile(t) // NS, 0, 0))

    in_specs = [pl.BlockSpec(memory_space=pltpu.SMEM), mixer_spec(D_MODEL), mlp_spec(PLE_DIM)]
    args = [sinks, x, p]
    if has_cache:
        in_specs += [per_seq(WINDOW, KV_WIDTH), per_seq(WINDOW, KV_WIDTH), per_seq(CONV_HIST, CONV_CH)]
        args += list(cache)
    in_specs += [_const_spec(w.shape) for w in weights]
    args += list(weights)

    out_shape = (
        jax.ShapeDtypeStruct(x.shape, jnp.float32),
        jax.ShapeDtypeStruct((B, KOUT, KV_WIDTH), jnp.float32),
        jax.ShapeDtypeStruct((B, KOUT, KV_WIDTH), jnp.float32),
        jax.ShapeDtypeStruct((B, CONV_HIST, CONV_CH), jnp.float32),
    )
    out_specs = (mlp_spec(D_MODEL), per_seq(KOUT, KV_WIDTH), per_seq(KOUT, KV_WIDTH),
                 per_seq(CONV_HIST, CONV_CH))
    scratch = [
        pltpu.VMEM((T, D_MODEL), jnp.bfloat16),
        pltpu.VMEM((T, ATTN_WIDTH), jnp.bfloat16),
        pltpu.VMEM((T, 2 * KV_WIDTH), jnp.float32),
        pltpu.VMEM((4, n_seq * (WINDOW + L), LANES), jnp.bfloat16),
        pltpu.VMEM((4, n_seq * (WINDOW + L), 2 * LANES), jnp.bfloat16),
        pltpu.VMEM((N_SLABS, n_seq * (CONV_HIST + L), LANES), jnp.float32),
        pltpu.VMEM((4 * n_ch, 2 * CH, win), jnp.bfloat16),
        pltpu.VMEM((2 * n_ch, 2 * CH, LANES), jnp.float32),
        pltpu.VMEM((N_SLABS, T, LANES), jnp.float32),
        pltpu.VMEM((T, D_MODEL), jnp.bfloat16),
        pltpu.VMEM((T, D_MODEL), jnp.float32),
        pltpu.VMEM((T, D_MODEL), jnp.bfloat16),
        pltpu.VMEM((T, D_MODEL), jnp.bfloat16),
        pltpu.VMEM((T, D_MODEL), jnp.float32),
        pltpu.VMEM((T, D_FF // FF_SPLIT), jnp.bfloat16),
    ]
    body = functools.partial(_layer_kernel, T=T, L=L, CH=CH, BLK=BLK, NS=NS, N_TILES=n_tiles, KOUT=KOUT,
                             has_cache=has_cache)
    y, nk, nv, nc = pl.pallas_call(
        body,
        grid=(n_tiles + 1,),
        in_specs=in_specs,
        out_specs=out_specs,
        out_shape=out_shape,
        scratch_shapes=scratch,
        compiler_params=pltpu.CompilerParams(
            dimension_semantics=("arbitrary",),
            vmem_limit_bytes=VMEM_LIMIT_BYTES),
        name=name,
    )(*args)
    return y.reshape(B, S, D_MODEL), nk, nv, nc


def _pick_tile(S, target):
    t = min(S, target)
    while S % t:
        t //= 2
    return t


def kernel(x_prompt, x_sample, p_prompt, p_sample, cache_k, cache_v, state_conv, norm_mix, w_in, sinks, conv_w, conv_b, ln_g, ln_b, attn_out_g, conv_out_g, w_out, norm_ffn, w_up, w_down, norm_ple, w_ple_gate, w_ple, final_norm):
    depth = w_in.shape[0]
    assert depth == 1, "single-layer step"
    row = lambda v: v.reshape(1, -1)
    conv_w8 = jnp.broadcast_to(conv_w[0][:, None, :], (CONV_KERNEL, SUBLANES, CONV_CH))
    bf = jnp.bfloat16
    weights = (row(norm_mix[0]), w_in[0].astype(bf), conv_w8, row(conv_b[0]), row(ln_g[0]), row(ln_b[0]),
               row(attn_out_g[0]), row(conv_out_g[0]), w_out[0].astype(bf), row(norm_ffn[0]),
               w_up[0].astype(bf), w_down[0].astype(bf), row(norm_ple[0]), w_ple_gate[0].astype(bf),
               w_ple[0].astype(bf), row(final_norm))
    sk = sinks[0]

    Bp, Sp, _ = x_prompt.shape
    yp, nkp, nvp, ncp = _layer_call(x_prompt, p_prompt[0], None, weights, sk,
                                    L=_pick_tile(Sp, 512), n_seq=1, CH=64, name="prompt_layer")

    Bs, Ss, _ = x_sample.shape
    assert cache_k.shape[2] == WINDOW
    cache = (cache_k[0].reshape(Bs, WINDOW, KV_WIDTH), cache_v[0].reshape(Bs, WINDOW, KV_WIDTH),
             jnp.pad(state_conv[0], ((0, 0), (CONV_PAD, 0), (0, 0))))
    ys, nks, nvs, ncs = _layer_call(x_sample, p_sample[0], cache, weights, sk,
                                    L=Ss, n_seq=Bs, CH=Ss, name="sample_layer")

    kv_shape = lambda a: a.reshape(1, a.shape[0], a.shape[1], 2, HEAD_DIM)
    return (yp, ys, kv_shape(nkp), kv_shape(nvp), ncp[None, :, CONV_PAD:, :],
            kv_shape(nks), kv_shape(nvs), ncs[None, :, CONV_PAD:, :])
```

```python
import functools

import jax
import jax.numpy as jnp
from jax import lax
from jax.experimental import pallas as pl
from jax.experimental.pallas import tpu as pltpu

D_MODEL = 1024
HEAD_DIM = 64
ATTN_WIDTH = 512
KV_WIDTH = 128
WINDOW = 128
CONV_CH = 512
CONV_KERNEL = 31
CONV_HIST = 32
CONV_PAD = CONV_HIST - (CONV_KERNEL - 1)
D_FF = 4096
PLE_DIM = 256
EPS = 1e-6
Q_OFF, K_OFF, A_OFF = 0, 512, 768
LANES = 128
SUBLANES = 8
N_SLABS = CONV_CH // LANES
FF_BLOCK = 1024
FF_SPLIT = 2
VMEM_LIMIT_BYTES = 56 * 1024 * 1024


def _rms(x, g):
    ms = jnp.mean(x * x, axis=-1, keepdims=True)
    return x * lax.rsqrt(ms + EPS) * g


def _sigmoid(x):
    return 1.0 / (1.0 + jnp.exp(-x))


def _kv_variants(k, v):
    rows = k.shape[0]
    lane = lax.broadcasted_iota(jnp.int32, (rows, LANES), 1)
    lo = lane < HEAD_DIM
    k_sw = pltpu.roll(k, HEAD_DIM, axis=1)
    v_sw = pltpu.roll(v, HEAD_DIM, axis=1)
    zero = jnp.zeros_like(k)
    one = jnp.ones_like(k)
    ones_lo = jnp.where(lo, one, zero).astype(jnp.bfloat16)
    ones_hi = jnp.where(lo, zero, one).astype(jnp.bfloat16)
    kz, vz = [], []
    for h in range(2):
        for half in range(2):
            src_k = k if h == half else k_sw
            src_v = v if h == half else v_sw
            keep = lo if half == 0 else jnp.logical_not(lo)
            kz.append(jnp.where(keep, src_k, zero).astype(jnp.bfloat16))
            vpart = jnp.where(keep, src_v, zero).astype(jnp.bfloat16)
            vz.append(jnp.concatenate([vpart, ones_lo if half == 0 else ones_hi], axis=1))
    return kz, vz


def _layer_kernel(*refs, T, L, CH, BLK, NS, N_TILES, KOUT, has_cache):
    it = iter(refs)
    sinks_ref = next(it)
    x_ref = next(it)
    p_ref = next(it)
    if has_cache:
        ck_ref = next(it)
        cv_ref = next(it)
        cs_ref = next(it)
    (norm_mix_ref, w_in_ref, conv_w_ref, conv_b_ref, ln_g_ref, ln_b_ref, attn_g_ref, conv_g_ref,
     w_out_ref, norm_ffn_ref, w_up_ref, w_down_ref, norm_ple_ref, w_gate_ref, w_ple_ref,
     final_norm_ref) = [next(it) for _ in range(16)]
    y_ref, nk_ref, nv_ref, nc_ref = [next(it) for _ in range(4)]
    (hn_ref, q_ref, kv_ref, kz_ref, vz_ref, ubuf_ref, e_ref, tt_ref, cbuf_ref, merged_ref, x1new_ref, hn2new_ref,
     hn2_ref, x1_ref, hact_ref) = [next(it) for _ in range(15)]

    t_idx = pl.program_id(0)
    s_idx = lax.rem(jnp.minimum(t_idx, N_TILES - 1), NS)
    win = WINDOW + CH
    grp = min(32, BLK, L)
    stride = grp // SUBLANES
    n_seq = T // L
    kv_rows = WINDOW + L
    conv_rows = CONV_HIST + L

    @pl.when(s_idx == 0)
    def _init_history():
        for q in range(n_seq):
            kb, cb = q * kv_rows, q * conv_rows
            if has_cache:
                kz, vz = _kv_variants(ck_ref[q], cv_ref[q])
                for i in range(4):
                    kz_ref[i, kb:kb + WINDOW, :] = kz[i]
                    vz_ref[i, kb:kb + WINDOW, :] = vz[i]
                for sl in range(N_SLABS):
                    ubuf_ref[sl, cb:cb + CONV_HIST, :] = cs_ref[q, :, sl * LANES:(sl + 1) * LANES]
            else:
                for i in range(4):
                    kz_ref[i, kb:kb + WINDOW, :] = jnp.zeros((WINDOW, LANES), jnp.bfloat16)
                    vz_ref[i, kb:kb + WINDOW, :] = jnp.zeros((WINDOW, 2 * LANES), jnp.bfloat16)
                for sl in range(N_SLABS):
                    ubuf_ref[sl, cb:cb + CONV_HIST, :] = jnp.zeros((CONV_HIST, LANES), jnp.float32)

    @pl.when(t_idx == 0)
    def _init_handoff():
        x1_ref[...] = jnp.zeros(x1_ref.shape, x1_ref.dtype)
        hn2_ref[...] = jnp.zeros(hn2_ref.shape, hn2_ref.dtype)

    lane = lax.broadcasted_iota(jnp.int32, (2 * CH, LANES), 1)
    lo = lane < HEAD_DIM
    key_idx = lax.broadcasted_iota(jnp.int32, (2 * CH, win), 1)
    top_rows = lax.broadcasted_iota(jnp.int32, (2 * CH, 1), 0) < CH

    def _norm_to(dst_ref, src_ref_rows, g_ref, r0, n):
        for b0 in range(r0, r0 + n, BLK):
            dst_ref[b0:b0 + BLK, :] = _rms(src_ref_rows(b0), g_ref[...]).astype(jnp.bfloat16)

    def _project(r0, n):
        r = slice(r0, r0 + n)
        q_ref[r, :] = (jnp.dot(hn_ref[r, :], w_in_ref[:, Q_OFF:Q_OFF + ATTN_WIDTH],
                               preferred_element_type=jnp.float32) * (HEAD_DIM ** -0.5)).astype(jnp.bfloat16)
        kv_ref[r, :] = jnp.dot(hn_ref[r, :], w_in_ref[:, K_OFF:K_OFF + 2 * KV_WIDTH],
                               preferred_element_type=jnp.float32)
        ag = jnp.dot(hn_ref[r, :], w_in_ref[:, A_OFF:], preferred_element_type=jnp.float32)
        for sl in range(N_SLABS):
            a = ag[:, sl * LANES:(sl + 1) * LANES]
            g = ag[:, CONV_CH + sl * LANES:CONV_CH + (sl + 1) * LANES]
            u = a * _sigmoid(g)
            for q in range(n_seq):
                cb = q * conv_rows + CONV_HIST
                ubuf_ref[sl, cb:cb + L, :] = u[q * L:(q + 1) * L]
        rows = min(BLK, L)
        for b0 in range(r0, r0 + n, rows):
            kz, vz = _kv_variants(kv_ref[b0:b0 + rows, 0:KV_WIDTH], kv_ref[b0:b0 + rows, KV_WIDTH:])
            kb = (b0 // L) * kv_rows + WINDOW + b0 % L
            for j in range(4):
                kz_ref[j, kb:kb + rows, :] = kz[j]
                vz_ref[j, kb:kb + rows, :] = vz[j]

    def _scores(c):
        r0 = c * CH
        w0 = (r0 // L) * kv_rows + r0 % L
        q_c = q_ref[r0:r0 + CH, :]
        if not has_cache:
            valid = key_idx >= WINDOW - (s_idx * T + r0)
        for h in range(2):
            lhs = jnp.concatenate([q_c[:, (2 * h) * LANES:(2 * h + 1) * LANES],
                                   q_c[:, (2 * h + 1) * LANES:(2 * h + 2) * LANES]], axis=0)
            sink_terms = []
            for half in range(2):
                s = lax.dot_general(lhs, kz_ref[2 * h + half, w0:w0 + win, :],
                                    (((1,), (1,)), ((), ())), preferred_element_type=jnp.float32)
                if not has_cache:
                    s = jnp.where(valid, s, -jnp.inf)
                sink = jnp.where(top_rows, sinks_ref[4 * h + half], sinks_ref[4 * h + 2 + half])
                m = jnp.maximum(jnp.max(s, axis=-1, keepdims=True), sink)
                e_ref[(c * 2 + h) * 2 + half] = jnp.exp(s - m).astype(jnp.bfloat16)
                sink_terms.append(jnp.exp(sink - m))
            tt_ref[c * 2 + h] = jnp.where(lo, sink_terms[0], sink_terms[1])

    def _conv(b0, sl):
        cols = slice(sl * LANES, (sl + 1) * LANES)
        starts = [b0 + g0 + v for g0 in range(0, BLK, grp) for v in range(stride)]
        acc = [jnp.zeros((SUBLANES, LANES), jnp.float32) for _ in starts]
        srcs = [(st // L) * conv_rows + st % L + CONV_PAD for st in starts]
        for j in range(CONV_KERNEL):
            w = conv_w_ref[j, :, cols]
            for n, src in enumerate(srcs):
                acc[n] = acc[n] + w * ubuf_ref[sl, pl.ds(src + j, SUBLANES, stride=stride), :]
        bias = conv_b_ref[:, cols]
        for n, st in enumerate(starts):
            cbuf_ref[sl, pl.ds(st, SUBLANES, stride=stride), :] = acc[n] + bias

    def _pv(c):
        r0 = c * CH
        w0 = (r0 // L) * kv_rows + r0 % L
        blocks = [None] * 4
        for h in range(2):
            acc = None
            for half in range(2):
                part = jnp.dot(e_ref[(c * 2 + h) * 2 + half], vz_ref[2 * h + half, w0:w0 + win, :],
                               preferred_element_type=jnp.float32)
                acc = part if acc is None else acc + part
            o = acc[:, :LANES] / (acc[:, LANES:] + tt_ref[c * 2 + h])
            blocks[2 * h] = o[:CH]
            blocks[2 * h + 1] = o[CH:]
        attn = jnp.concatenate(blocks, axis=1)
        merged_ref[r0:r0 + CH, :ATTN_WIDTH] = _rms(attn, attn_g_ref[...]).astype(jnp.bfloat16)

    def _ln(b0):
        r = slice(b0, b0 + BLK)
        cv = jnp.concatenate([cbuf_ref[sl, r, :] for sl in range(N_SLABS)], axis=1)
        mu = jnp.mean(cv, axis=-1, keepdims=True)
        xc = cv - mu
        y = xc * lax.rsqrt(jnp.mean(xc * xc, axis=-1, keepdims=True) + EPS) * ln_g_ref[...] + ln_b_ref[...]
        y = y * _sigmoid(y)
        merged_ref[r, ATTN_WIDTH:] = _rms(y, conv_g_ref[...]).astype(jnp.bfloat16)

    def _wout():
        x1new_ref[...] = x_ref[0] + jnp.dot(merged_ref[...], w_out_ref[...], preferred_element_type=jnp.float32)

    ff_group = D_FF // FF_SPLIT

    def _up(f):
        hid = jnp.dot(hn2_ref[...], w_up_ref[:, f * FF_BLOCK:(f + 1) * FF_BLOCK],
                      preferred_element_type=jnp.float32)
        hid = jnp.maximum(hid, 0.0)
        col = (f * FF_BLOCK) % ff_group
        hact_ref[:, col:col + FF_BLOCK] = (hid * hid).astype(jnp.bfloat16)

    def _down(fs):
        x1_ref[...] = x1_ref[...] + jnp.dot(hact_ref[...], w_down_ref[fs * ff_group:(fs + 1) * ff_group, :],
                                            preferred_element_type=jnp.float32)

    def _gate_ple():
        gate = _sigmoid(jnp.dot(hn2_ref[...], w_gate_ref[...], preferred_element_type=jnp.float32))
        ple = jnp.dot(p_ref[0].astype(jnp.bfloat16), w_ple_ref[...], preferred_element_type=jnp.float32)
        x1_ref[...] = x1_ref[...] + ple * gate

    def _norm_out():
        for b0 in range(0, T, BLK):
            y_ref[0, b0:b0 + BLK, :] = _rms(x1_ref[b0:b0 + BLK, :], final_norm_ref[...])

    mixer_units = [functools.partial(_scores, c) for c in range(T // CH)]
    for b0 in range(0, T, BLK):
        mixer_units += [functools.partial(_conv, b0, sl) for sl in range(N_SLABS)]
        mixer_units += [functools.partial(_pv, c) for c in range(b0 // CH, (b0 + BLK) // CH)]
        mixer_units.append(functools.partial(_ln, b0))
    per_group = ff_group // FF_BLOCK
    mlp_units = []
    for fs in range(FF_SPLIT):
        mlp_units += [functools.partial(_up, fs * per_group + f) for f in range(per_group)]
        mlp_units.append(functools.partial(_down, fs))

    head = 2
    for mlp_unit in mlp_units[:head]:
        mlp_unit()
    _norm_to(hn_ref, lambda b0: x_ref[0, b0:b0 + BLK, :], norm_mix_ref, 0, T)
    _project(0, T)
    done = 0
    for k, mlp_unit in enumerate(mlp_units[head:]):
        upto = (k + 1) * len(mixer_units) // (len(mlp_units) - head)
        for unit in mixer_units[done:upto]:
            unit()
        done = upto
        mlp_unit()
    _wout()
    _norm_to(hn2_ref, lambda b0: x1_ref[b0:b0 + BLK, :], norm_ple_ref, 0, T)
    _norm_to(hn2new_ref, lambda b0: x1new_ref[b0:b0 + BLK, :], norm_ffn_ref, 0, T)
    _gate_ple()
    _norm_out()
    x1_ref[...] = x1new_ref[...]
    hn2_ref[...] = hn2new_ref[...]

    @pl.when(s_idx == NS - 1)
    def _emit_state():
        for q in range(n_seq):
            last = slice((q + 1) * L - KOUT, (q + 1) * L)
            nk_ref[q] = kv_ref[last, 0:KV_WIDTH]
            nv_ref[q] = kv_ref[last, KV_WIDTH:]
            cb = q * conv_rows + L
            nc_ref[q] = jnp.concatenate([ubuf_ref[sl, cb:cb + CONV_HIST, :] for sl in range(N_SLABS)], axis=1)

    if NS > 1:
        for i in range(4):
            kz_ref[i, 0:WINDOW, :] = kz_ref[i, T:T + WINDOW, :]
            vz_ref[i, 0:WINDOW, :] = vz_ref[i, T:T + WINDOW, :]
        for sl in range(N_SLABS):
            ubuf_ref[sl, 0:CONV_HIST, :] = ubuf_ref[sl, T:T + CONV_HIST, :]


def _const_spec(shape):
    nd = len(shape)
    return pl.BlockSpec(shape, lambda t: (0,) * nd, pipeline_mode=pl.Buffered(1))


def _layer_call(x, p, cache, weights, sinks, *, L, n_seq, CH, name):
    B, S, _ = x.shape
    T = n_seq * L
    NS = S // L
    BLK = min(128, T)
    KOUT = min(WINDOW, S)
    has_cache = cache is not None
    assert S % L == 0 and B % n_seq == 0 and (n_seq == 1 or NS == 1)
    assert T % BLK == 0 and BLK % CH == 0 and L % CH == 0 and (L % BLK == 0 or BLK % L == 0) and KOUT <= L
    n_ch = T // CH
    win = WINDOW + CH
    n_tiles = (B // n_seq) * NS
    x = x.reshape(B // n_seq, n_seq * S, D_MODEL)
    p = p.reshape(B // n_seq, n_seq * S, PLE_DIM)

    def mixer_tile(t):
        return jnp.minimum(t, n_tiles - 1)

    def mlp_tile(t):
        return jnp.maximum(t - 1, 0)

    mixer_spec = lambda width: pl.BlockSpec((1, T, width), lambda t: (mixer_tile(t) // NS, mixer_tile(t) % NS, 0))
    mlp_spec = lambda width: pl.BlockSpec((1, T, width), lambda t: (mlp_tile(t) // NS, mlp_tile(t) % NS, 0))
    per_seq = lambda rows, width: pl.BlockSpec((n_seq, rows, width), lambda t: (mixer_tile(t) // NS, 0, 0))

    in_specs = [pl.BlockSpec(memory_space=pltpu.SMEM), mixer_spec(D_MODEL), mlp_spec(PLE_DIM)]
    args = [sinks, x, p]
    if has_cache:
        in_specs += [per_seq(WINDOW, KV_WIDTH), per_seq(WINDOW, KV_WIDTH), per_seq(CONV_HIST, CONV_CH)]
        args += list(cache)
    in_specs += [_const_spec(w.shape) for w in weights]
    args += list(weights)

    out_shape = (
        jax.ShapeDtypeStruct(x.shape, jnp.float32),
        jax.ShapeDtypeStruct((B, KOUT, KV_WIDTH), jnp.float32),
        jax.ShapeDtypeStruct((B, KOUT, KV_WIDTH), jnp.float32),
        jax.ShapeDtypeStruct((B, CONV_HIST, CONV_CH), jnp.float32),
    )
    out_specs = (mlp_spec(D_MODEL), per_seq(KOUT, KV_WIDTH), per_seq(KOUT, KV_WIDTH),
                 per_seq(CONV_HIST, CONV_CH))
    scratch = [
        pltpu.VMEM((T, D_MODEL), jnp.bfloat16),
        pltpu.VMEM((T, ATTN_WIDTH), jnp.bfloat16),
        pltpu.VMEM((T, 2 * KV_WIDTH), jnp.float32),
        pltpu.VMEM((4, n_seq * (WINDOW + L), LANES), jnp.bfloat16),
        pltpu.VMEM((4, n_seq * (WINDOW + L), 2 * LANES), jnp.bfloat16),
        pltpu.VMEM((N_SLABS, n_seq * (CONV_HIST + L), LANES), jnp.float32),
        pltpu.VMEM((4 * n_ch, 2 * CH, win), jnp.bfloat16),
        pltpu.VMEM((2 * n_ch, 2 * CH, LANES), jnp.float32),
        pltpu.VMEM((N_SLABS, T, LANES), jnp.float32),
        pltpu.VMEM((T, D_MODEL), jnp.bfloat16),
        pltpu.VMEM((T, D_MODEL), jnp.float32),
        pltpu.VMEM((T, D_MODEL), jnp.bfloat16),
        pltpu.VMEM((T, D_MODEL), jnp.bfloat16),
        pltpu.VMEM((T, D_MODEL), jnp.float32),
        pltpu.VMEM((T, D_FF // FF_SPLIT), jnp.bfloat16),
    ]
    body = functools.partial(_layer_kernel, T=T, L=L, CH=CH, BLK=BLK, NS=NS, N_TILES=n_tiles, KOUT=KOUT,
                             has_cache=has_cache)
    y, nk, nv, nc = pl.pallas_call(
        body,
        grid=(n_tiles + 1,),
        in_specs=in_specs,
        out_specs=out_specs,
        out_shape=out_shape,
        scratch_shapes=scratch,
        compiler_params=pltpu.CompilerParams(
            dimension_semantics=("arbitrary",),
            vmem_limit_bytes=VMEM_LIMIT_BYTES),
        name=name,
    )(*args)
    return y.reshape(B, S, D_MODEL), nk, nv, nc


def _pick_tile(S, target):
    t = min(S, target)
    while S % t:
        t //= 2
    return t


def kernel(x_prompt, x_sample, p_prompt, p_sample, cache_k, cache_v, state_conv, norm_mix, w_in, sinks, conv_w, conv_b, ln_g, ln_b, attn_out_g, conv_out_g, w_out, norm_ffn, w_up, w_down, norm_ple, w_ple_gate, w_ple, final_norm):
    depth = w_in.shape[0]
    assert depth == 1, "single-layer step"
    row = lambda v: v.reshape(1, -1)
    conv_w8 = jnp.broadcast_to(conv_w[0][:, None, :], (CONV_KERNEL, SUBLANES, CONV_CH))
    bf = jnp.bfloat16
    weights = (row(norm_mix[0]), w_in[0].astype(bf), conv_w8, row(conv_b[0]), row(ln_g[0]), row(ln_b[0]),
               row(attn_out_g[0]), row(conv_out_g[0]), w_out[0].astype(bf), row(norm_ffn[0]),
               w_up[0].astype(bf), w_down[0].astype(bf), row(norm_ple[0]), w_ple_gate[0].astype(bf),
               w_ple[0].astype(bf), row(final_norm))
    sk = sinks[0]

    Bp, Sp, _ = x_prompt.shape
    yp, nkp, nvp, ncp = _layer_call(x_prompt, p_prompt[0], None, weights, sk,
                                    L=_pick_tile(Sp, 512), n_seq=1, CH=64, name="prompt_layer")

    Bs, Ss, _ = x_sample.shape
    assert cache_k.shape[2] == WINDOW
    cache = (cache_k[0].reshape(Bs, WINDOW, KV_WIDTH), cache_v[0].reshape(Bs, WINDOW, KV_WIDTH),
             jnp.pad(state_conv[0], ((0, 0), (CONV_PAD, 0), (0, 0))))
    ys, nks, nvs, ncs = _layer_call(x_sample, p_sample[0], cache, weights, sk,
                                    L=Ss, n_seq=Bs, CH=Ss, name="sample_layer")

    kv_shape = lambda a: a.reshape(1, a.shape[0], a.shape[1], 2, HEAD_DIM)
    return (yp, ys, kv_shape(nkp), kv_shape(nvp), ncp[None, :, CONV_PAD:, :],
            kv_shape(nks), kv_shape(nvs), ncs[None, :, CONV_PAD:, :])
```

```python
import functools

import jax
import jax.numpy as jnp
from jax import lax
from jax.experimental import pallas as pl
from jax.experimental.pallas import tpu as pltpu

D_MODEL = 1024
HEAD_DIM = 64
ATTN_WIDTH = 512
KV_WIDTH = 128
WINDOW = 128
CONV_CH = 512
CONV_KERNEL = 31
CONV_HIST = 32
CONV_PAD = CONV_HIST - (CONV_KERNEL - 1)
D_FF = 4096
PLE_DIM = 256
EPS = 1e-6
Q_OFF, K_OFF, A_OFF = 0, 512, 768
LANES = 128
SUBLANES = 8
N_SLABS = CONV_CH // LANES
FF_BLOCK = 512
FF_SPLIT = 2
VMEM_LIMIT_BYTES = 56 * 1024 * 1024


def _rms(x, g):
    ms = jnp.mean(x * x, axis=-1, keepdims=True)
    return x * lax.rsqrt(ms + EPS) * g


def _sigmoid(x):
    return 1.0 / (1.0 + jnp.exp(-x))


def _kv_variants(k, v):
    rows = k.shape[0]
    lane = lax.broadcasted_iota(jnp.int32, (rows, LANES), 1)
    lo = lane < HEAD_DIM
    k_sw = pltpu.roll(k, HEAD_DIM, axis=1)
    v_sw = pltpu.roll(v, HEAD_DIM, axis=1)
    zero = jnp.zeros_like(k)
    one = jnp.ones_like(k)
    ones_lo = jnp.where(lo, one, zero).astype(jnp.bfloat16)
    ones_hi = jnp.where(lo, zero, one).astype(jnp.bfloat16)
    kz, vz = [], []
    for h in range(2):
        for half in range(2):
            src_k = k if h == half else k_sw
            src_v = v if h == half else v_sw
            keep = lo if half == 0 else jnp.logical_not(lo)
            kz.append(jnp.where(keep, src_k, zero).astype(jnp.bfloat16))
            vpart = jnp.where(keep, src_v, zero).astype(jnp.bfloat16)
            vz.append(jnp.concatenate([vpart, ones_lo if half == 0 else ones_hi], axis=1))
    return kz, vz


def _layer_kernel(*refs, T, L, CH, BLK, NS, N_TILES, KOUT, has_cache):
    it = iter(refs)
    sinks_ref = next(it)
    x_ref = next(it)
    p_ref = next(it)
    if has_cache:
        ck_ref = next(it)
        cv_ref = next(it)
        cs_ref = next(it)
    (norm_mix_ref, w_in_ref, conv_w_ref, conv_b_ref, ln_g_ref, ln_b_ref, attn_g_ref, conv_g_ref,
     w_out_ref, norm_ffn_ref, w_up_ref, w_down_ref, norm_ple_ref, w_gate_ref, w_ple_ref,
     final_norm_ref) = [next(it) for _ in range(16)]
    y_ref, nk_ref, nv_ref, nc_ref = [next(it) for _ in range(4)]
    (hn_ref, q_ref, kv_ref, kz_ref, vz_ref, ubuf_ref, e_ref, tt_ref, cbuf_ref, merged_ref, x1new_ref, hn2new_ref,
     hn2_ref, x1_ref, hact_ref) = [next(it) for _ in range(15)]

    t_idx = pl.program_id(0)
    s_idx = lax.rem(jnp.minimum(t_idx, N_TILES - 1), NS)
    win = WINDOW + CH
    grp = min(32, BLK, L)
    stride = grp // SUBLANES
    n_seq = T // L
    kv_rows = WINDOW + L
    conv_rows = CONV_HIST + L

    @pl.when(s_idx == 0)
    def _init_history():
        for q in range(n_seq):
            kb, cb = q * kv_rows, q * conv_rows
            if has_cache:
                kz, vz = _kv_variants(ck_ref[q], cv_ref[q])
                for i in range(4):
                    kz_ref[i, kb:kb + WINDOW, :] = kz[i]
                    vz_ref[i, kb:kb + WINDOW, :] = vz[i]
                for sl in range(N_SLABS):
                    ubuf_ref[sl, cb:cb + CONV_HIST, :] = cs_ref[q, :, sl * LANES:(sl + 1) * LANES]
            else:
                for i in range(4):
                    kz_ref[i, kb:kb + WINDOW, :] = jnp.zeros((WINDOW, LANES), jnp.bfloat16)
                    vz_ref[i, kb:kb + WINDOW, :] = jnp.zeros((WINDOW, 2 * LANES), jnp.bfloat16)
                for sl in range(N_SLABS):
                    ubuf_ref[sl, cb:cb + CONV_HIST, :] = jnp.zeros((CONV_HIST, LANES), jnp.float32)

    @pl.when(t_idx == 0)
    def _init_handoff():
        x1_ref[...] = jnp.zeros(x1_ref.shape, x1_ref.dtype)
        hn2_ref[...] = jnp.zeros(hn2_ref.shape, hn2_ref.dtype)

    lane = lax.broadcasted_iota(jnp.int32, (2 * CH, LANES), 1)
    lo = lane < HEAD_DIM
    key_idx = lax.broadcasted_iota(jnp.int32, (2 * CH, win), 1)
    top_rows = lax.broadcasted_iota(jnp.int32, (2 * CH, 1), 0) < CH

    def _norm_to(dst_ref, src_ref_rows, g_ref, r0, n):
        for b0 in range(r0, r0 + n, BLK):
            dst_ref[b0:b0 + BLK, :] = _rms(src_ref_rows(b0), g_ref[...]).astype(jnp.bfloat16)

    def _project(r0, n):
        r = slice(r0, r0 + n)
        q_ref[r, :] = (jnp.dot(hn_ref[r, :], w_in_ref[:, Q_OFF:Q_OFF + ATTN_WIDTH],
                               preferred_element_type=jnp.float32) * (HEAD_DIM ** -0.5)).astype(jnp.bfloat16)
        kv_ref[r, :] = jnp.dot(hn_ref[r, :], w_in_ref[:, K_OFF:K_OFF + 2 * KV_WIDTH],
                               preferred_element_type=jnp.float32)
        ag = jnp.dot(hn_ref[r, :], w_in_ref[:, A_OFF:], preferred_element_type=jnp.float32)
        for sl in range(N_SLABS):
            a = ag[:, sl * LANES:(sl + 1) * LANES]
            g = ag[:, CONV_CH + sl * LANES:CONV_CH + (sl + 1) * LANES]
            u = a * _sigmoid(g)
            for q in range(n_seq):
                cb = q * conv_rows + CONV_HIST
                ubuf_ref[sl, cb:cb + L, :] = u[q * L:(q + 1) * L]
        rows = min(BLK, L)
        for b0 in range(r0, r0 + n, rows):
            kz, vz = _kv_variants(kv_ref[b0:b0 + rows, 0:KV_WIDTH], kv_ref[b0:b0 + rows, KV_WIDTH:])
            kb = (b0 // L) * kv_rows + WINDOW + b0 % L
            for j in range(4):
                kz_ref[j, kb:kb + rows, :] = kz[j]
                vz_ref[j, kb:kb + rows, :] = vz[j]

    def _scores(c):
        r0 = c * CH
        w0 = (r0 // L) * kv_rows + r0 % L
        q_c = q_ref[r0:r0 + CH, :]
        if not has_cache:
            valid = key_idx >= WINDOW - (s_idx * T + r0)
        for h in range(2):
            lhs = jnp.concatenate([q_c[:, (2 * h) * LANES:(2 * h + 1) * LANES],
                                   q_c[:, (2 * h + 1) * LANES:(2 * h + 2) * LANES]], axis=0)
            sink_terms = []
            for half in range(2):
                s = lax.dot_general(lhs, kz_ref[2 * h + half, w0:w0 + win, :],
                                    (((1,), (1,)), ((), ())), preferred_element_type=jnp.float32)
                if not has_cache:
                    s = jnp.where(valid, s, -jnp.inf)
                sink = jnp.where(top_rows, sinks_ref[4 * h + half], sinks_ref[4 * h + 2 + half])
                m = jnp.maximum(jnp.max(s, axis=-1, keepdims=True), sink)
                e_ref[(c * 2 + h) * 2 + half] = jnp.exp(s - m).astype(jnp.bfloat16)
                sink_terms.append(jnp.exp(sink - m))
            tt_ref[c * 2 + h] = jnp.where(lo, sink_terms[0], sink_terms[1])

    def _conv(b0, sl):
        cols = slice(sl * LANES, (sl + 1) * LANES)
        starts = [b0 + g0 + v for g0 in range(0, BLK, grp) for v in range(stride)]
        acc = [jnp.zeros((SUBLANES, LANES), jnp.float32) for _ in starts]
        srcs = [(st // L) * conv_rows + st % L + CONV_PAD for st in starts]
        for j in range(CONV_KERNEL):
            w = conv_w_ref[j, :, cols]
            for n, src in enumerate(srcs):
                acc[n] = acc[n] + w * ubuf_ref[sl, pl.ds(src + j, SUBLANES, stride=stride), :]
        bias = conv_b_ref[:, cols]
        for n, st in enumerate(starts):
            cbuf_ref[sl, pl.ds(st, SUBLANES, stride=stride), :] = acc[n] + bias

    def _pv(c):
        r0 = c * CH
        w0 = (r0 // L) * kv_rows + r0 % L
        blocks = [None] * 4
        for h in range(2):
            acc = None
            for half in range(2):
                part = jnp.dot(e_ref[(c * 2 + h) * 2 + half], vz_ref[2 * h + half, w0:w0 + win, :],
                               preferred_element_type=jnp.float32)
                acc = part if acc is None else acc + part
            o = acc[:, :LANES] / (acc[:, LANES:] + tt_ref[c * 2 + h])
            blocks[2 * h] = o[:CH]
            blocks[2 * h + 1] = o[CH:]
        attn = jnp.concatenate(blocks, axis=1)
        merged_ref[r0:r0 + CH, :ATTN_WIDTH] = _rms(attn, attn_g_ref[...]).astype(jnp.bfloat16)

    def _ln(b0):
        r = slice(b0, b0 + BLK)
        cv = jnp.concatenate([cbuf_ref[sl, r, :] for sl in range(N_SLABS)], axis=1)
        mu = jnp.mean(cv, axis=-1, keepdims=True)
        xc = cv - mu
        y = xc * lax.rsqrt(jnp.mean(xc * xc, axis=-1, keepdims=True) + EPS) * ln_g_ref[...] + ln_b_ref[...]
        y = y * _sigmoid(y)
        merged_ref[r, ATTN_WIDTH:] = _rms(y, conv_g_ref[...]).astype(jnp.bfloat16)

    def _wout():
        x1new_ref[...] = x_ref[0] + jnp.dot(merged_ref[...], w_out_ref[...], preferred_element_type=jnp.float32)

    ff_group = D_FF // FF_SPLIT

    def _up(f):
        hid = jnp.dot(hn2_ref[...], w_up_ref[:, f * FF_BLOCK:(f + 1) * FF_BLOCK],
                      preferred_element_type=jnp.float32)
        hid = jnp.maximum(hid, 0.0)
        col = (f * FF_BLOCK) % ff_group
        hact_ref[:, col:col + FF_BLOCK] = (hid * hid).astype(jnp.bfloat16)

    def _down(fs):
        x1_ref[...] = x1_ref[...] + jnp.dot(hact_ref[...], w_down_ref[fs * ff_group:(fs + 1) * ff_group, :],
                                            preferred_element_type=jnp.float32)

    def _gate_ple():
        gate = _sigmoid(jnp.dot(hn2_ref[...], w_gate_ref[...], preferred_element_type=jnp.float32))
        ple = jnp.dot(p_ref[0].astype(jnp.bfloat16), w_ple_ref[...], preferred_element_type=jnp.float32)
        x1_ref[...] = x1_ref[...] + ple * gate

    def _norm_out():
        for b0 in range(0, T, BLK):
            y_ref[0, b0:b0 + BLK, :] = _rms(x1_ref[b0:b0 + BLK, :], final_norm_ref[...])

    mixer_units = [functools.partial(_scores, c) for c in range(T // CH)]
    for b0 in range(0, T, BLK):
        mixer_units += [functools.partial(_conv, b0, sl) for sl in range(N_SLABS)]
        mixer_units += [functools.partial(_pv, c) for c in range(b0 // CH, (b0 + BLK) // CH)]
        mixer_units.append(functools.partial(_ln, b0))
    per_group = ff_group // FF_BLOCK
    mlp_units = []
    for fs in range(FF_SPLIT):
        mlp_units += [functools.partial(_up, fs * per_group + f) for f in range(per_group)]
        mlp_units.append(functools.partial(_down, fs))

    mlp_units[0]()
    _norm_to(hn_ref, lambda b0: x_ref[0, b0:b0 + BLK, :], norm_mix_ref, 0, T)
    _project(0, T)
    done = 0
    for k, mlp_unit in enumerate(mlp_units[1:]):
        upto = (k + 1) * len(mixer_units) // (len(mlp_units) - 1)
        for unit in mixer_units[done:upto]:
            unit()
        done = upto
        mlp_unit()
    _wout()
    _norm_to(hn2_ref, lambda b0: x1_ref[b0:b0 + BLK, :], norm_ple_ref, 0, T)
    _norm_to(hn2new_ref, lambda b0: x1new_ref[b0:b0 + BLK, :], norm_ffn_ref, 0, T)
    _gate_ple()
    _norm_out()
    x1_ref[...] = x1new_ref[...]
    hn2_ref[...] = hn2new_ref[...]

    @pl.when(s_idx == NS - 1)
    def _emit_state():
        for q in range(n_seq):
            last = slice((q + 1) * L - KOUT, (q + 1) * L)
            nk_ref[q] = kv_ref[last, 0:KV_WIDTH]
            nv_ref[q] = kv_ref[last, KV_WIDTH:]
            cb = q * conv_rows + L
            nc_ref[q] = jnp.concatenate([ubuf_ref[sl, cb:cb + CONV_HIST, :] for sl in range(N_SLABS)], axis=1)

    if NS > 1:
        for i in range(4):
            kz_ref[i, 0:WINDOW, :] = kz_ref[i, T:T + WINDOW, :]
            vz_ref[i, 0:WINDOW, :] = vz_ref[i, T:T + WINDOW, :]
        for sl in range(N_SLABS):
            ubuf_ref[sl, 0:CONV_HIST, :] = ubuf_ref[sl, T:T + CONV_HIST, :]


def _const_spec(shape):
    nd = len(shape)
    return pl.BlockSpec(shape, lambda t: (0,) * nd, pipeline_mode=pl.Buffered(1))


def _layer_call(x, p, cache, weights, sinks, *, L, n_seq, CH, name):
    B, S, _ = x.shape
    T = n_seq * L
    NS = S // L
    BLK = min(128, T)
    KOUT = min(WINDOW, S)
    has_cache = cache is not None
    assert S % L == 0 and B % n_seq == 0 and (n_seq == 1 or NS == 1)
    assert T % BLK == 0 and BLK % CH == 0 and L % CH == 0 and (L % BLK == 0 or BLK % L == 0) and KOUT <= L
    n_ch = T // CH
    win = WINDOW + CH
    n_tiles = (B // n_seq) * NS
    x = x.reshape(B // n_seq, n_seq * S, D_MODEL)
    p = p.reshape(B // n_seq, n_seq * S, PLE_DIM)

    def mixer_tile(t):
        return jnp.minimum(t, n_tiles - 1)

    def mlp_tile(t):
        return jnp.maximum(t - 1, 0)

    mixer_spec = lambda width: pl.BlockSpec((1, T, width), lambda t: (mixer_tile(t) // NS, mixer_tile(t) % NS, 0))
    mlp_spec = lambda width: pl.BlockSpec((1, T, width), lambda t: (mlp_tile(t) // NS, mlp_tile(t) % NS, 0))
    per_seq = lambda rows, width: pl.BlockSpec((n_seq, rows, width), lambda t: (mixer_tile(t) // NS, 0, 0))

    in_specs = [pl.BlockSpec(memory_space=pltpu.SMEM), mixer_spec(D_MODEL), mlp_spec(PLE_DIM)]
    args = [sinks, x, p]
    if has_cache:
        in_specs += [per_seq(WINDOW, KV_WIDTH), per_seq(WINDOW, KV_WIDTH), per_seq(CONV_HIST, CONV_CH)]
        args += list(cache)
    in_specs += [_const_spec(w.shape) for w in weights]
    args += list(weights)

    out_shape = (
        jax.ShapeDtypeStruct(x.shape, jnp.float32),
        jax.ShapeDtypeStruct((B, KOUT, KV_WIDTH), jnp.float32),
        jax.ShapeDtypeStruct((B, KOUT, KV_WIDTH), jnp.float32),
        jax.ShapeDtypeStruct((B, CONV_HIST, CONV_CH), jnp.float32),
    )
    out_specs = (mlp_spec(D_MODEL), per_seq(KOUT, KV_WIDTH), per_seq(KOUT, KV_WIDTH),
                 per_seq(CONV_HIST, CONV_CH))
    scratch = [
        pltpu.VMEM((T, D_MODEL), jnp.bfloat16),
        pltpu.VMEM((T, ATTN_WIDTH), jnp.bfloat16),
        pltpu.VMEM((T, 2 * KV_WIDTH), jnp.float32),
        pltpu.VMEM((4, n_seq * (WINDOW + L), LANES), jnp.bfloat16),
        pltpu.VMEM((4, n_seq * (WINDOW + L), 2 * LANES), jnp.bfloat16),
        pltpu.VMEM((N_SLABS, n_seq * (CONV_HIST + L), LANES), jnp.float32),
        pltpu.VMEM((4 * n_ch, 2 * CH, win), jnp.bfloat16),
        pltpu.VMEM((2 * n_ch, 2 * CH, LANES), jnp.float32),
        pltpu.VMEM((N_SLABS, T, LANES), jnp.float32),
        pltpu.VMEM((T, D_MODEL), jnp.bfloat16),
        pltpu.VMEM((T, D_MODEL), jnp.float32),
        pltpu.VMEM((T, D_MODEL), jnp.bfloat16),
        pltpu.VMEM((T, D_MODEL), jnp.bfloat16),
        pltpu.VMEM((T, D_MODEL), jnp.float32),
        pltpu.VMEM((T, D_FF // FF_SPLIT), jnp.bfloat16),
    ]
    body = functools.partial(_layer_kernel, T=T, L=L, CH=CH, BLK=BLK, NS=NS, N_TILES=n_tiles, KOUT=KOUT,
                             has_cache=has_cache)
    y, nk, nv, nc = pl.pallas_call(
        body,
        grid=(n_tiles + 1,),
        in_specs=in_specs,
        out_specs=out_specs,
        out_shape=out_shape,
        scratch_shapes=scratch,
        compiler_params=pltpu.CompilerParams(
            dimension_semantics=("arbitrary",),
            vmem_limit_bytes=VMEM_LIMIT_BYTES),
        name=name,
    )(*args)
    return y.reshape(B, S, D_MODEL), nk, nv, nc


def _pick_tile(S, target):
    t = min(S, target)
    while S % t:
        t //= 2
    return t


def kernel(x_prompt, x_sample, p_prompt, p_sample, cache_k, cache_v, state_conv, norm_mix, w_in, sinks, conv_w, conv_b, ln_g, ln_b, attn_out_g, conv_out_g, w_out, norm_ffn, w_up, w_down, norm_ple, w_ple_gate, w_ple, final_norm):
    depth = w_in.shape[0]
    assert depth == 1, "single-layer step"
    row = lambda v: v.reshape(1, -1)
    conv_w8 = jnp.broadcast_to(conv_w[0][:, None, :], (CONV_KERNEL, SUBLANES, CONV_CH))
    bf = jnp.bfloat16
    weights = (row(norm_mix[0]), w_in[0].astype(bf), conv_w8, row(conv_b[0]), row(ln_g[0]), row(ln_b[0]),
               row(attn_out_g[0]), row(conv_out_g[0]), w_out[0].astype(bf), row(norm_ffn[0]),
               w_up[0].astype(bf), w_down[0].astype(bf), row(norm_ple[0]), w_ple_gate[0].astype(bf),
               w_ple[0].astype(bf), row(final_norm))
    sk = sinks[0]

    Bp, Sp, _ = x_prompt.shape
    yp, nkp, nvp, ncp = _layer_call(x_prompt, p_prompt[0], None, weights, sk,
                                    L=_pick_tile(Sp, 512), n_seq=1, CH=64, name="prompt_layer")

    Bs, Ss, _ = x_sample.shape
    assert cache_k.shape[2] == WINDOW
    cache = (cache_k[0].reshape(Bs, WINDOW, KV_WIDTH), cache_v[0].reshape(Bs, WINDOW, KV_WIDTH),
             jnp.pad(state_conv[0], ((0, 0), (CONV_PAD, 0), (0, 0))))
    ys, nks, nvs, ncs = _layer_call(x_sample, p_sample[0], cache, weights, sk,
                                    L=Ss, n_seq=Bs, CH=Ss, name="sample_layer")

    kv_shape = lambda a: a.reshape(1, a.shape[0], a.shape[1], 2, HEAD_DIM)
    return (yp, ys, kv_shape(nkp), kv_shape(nvp), ncp[None, :, CONV_PAD:, :],
            kv_shape(nks), kv_shape(nvs), ncs[None, :, CONV_PAD:, :])
```
